```python
import math
import jax, jax.numpy as jnp
from jax import lax
import numpy as np

D_MODEL = 1024
BATCH = 8
SEQ = 2048
DEPTH = 4

CTX_LEN = 256
GRID_W = 64
SSD_HEADS = 16
SSD_HEAD_DIM = 64
SSD_DIM = SSD_HEADS * SSD_HEAD_DIM
SSD_GROUPS = 2
SSD_STATE = 128
SSD_CONV_W = 5
SSD_CHUNK = 128
XBC_DIM = SSD_DIM + 2 * SSD_GROUPS * SSD_STATE
CM_DIM = D_MODEL
CM_KERNEL = 31
MIX_DIM = SSD_DIM + CM_DIM
IN_DIM = SSD_DIM + XBC_DIM + 2 * SSD_HEADS + 2 * CM_DIM
FFN_HIDDEN = -(-8 * D_MODEL // (3 * 256)) * 256
EPS = 1e-6

kernel_name = "hybrid_ssd_conformer_flow_backbone"


def _rmsnorm(x, g):
    xf = x.astype(jnp.float32)
    y = xf * lax.rsqrt(jnp.mean(xf * xf, axis=-1, keepdims=True) + EPS)
    return y.astype(x.dtype) * g


def _layernorm(x, g, b):
    xf = x.astype(jnp.float32)
    mu = jnp.mean(xf, axis=-1, keepdims=True)
    xc = xf - mu
    var = jnp.mean(xc * xc, axis=-1, keepdims=True)
    return (xc * lax.rsqrt(var + EPS)).astype(x.dtype) * g + b


def _dwconv(x, w, b):
    k = w.shape[0]
    y = lax.conv_general_dilated(
        x, w[:, None, :].astype(x.dtype), window_strides=(1,),
        padding=[(k // 2, k // 2)], dimension_numbers=('NWC', 'WIO', 'NWC'),
        feature_group_count=x.shape[-1])
    return y + b


def _ssd_scan(xs, dt, a, bm, cm, h0):
    f32 = jnp.float32
    b, l, h, p = xs.shape
    g, n = bm.shape[2], bm.shape[3]
    nc = l // SSD_CHUNK
    rep = h // g
    xs, dt, bm, cm = xs.astype(f32), dt.astype(f32), bm.astype(f32), cm.astype(f32)
    bh = jnp.repeat(bm, rep, axis=2).reshape(b, nc, SSD_CHUNK, h, n)
    ch = jnp.repeat(cm, rep, axis=2).reshape(b, nc, SSD_CHUNK, h, n)
    xdt = (xs * dt[..., None]).reshape(b, nc, SSD_CHUNK, h, p)
    a_cs = jnp.cumsum((dt * a).reshape(b, nc, SSD_CHUNK, h), axis=2)
    mask = jnp.tril(jnp.ones((SSD_CHUNK, SSD_CHUNK), dtype=bool))[None, None, :, :, None]
    seg = a_cs[:, :, :, None, :] - a_cs[:, :, None, :, :]
    decay = jnp.exp(jnp.where(mask, seg, -jnp.inf))
    scores = jnp.einsum('bclhn,bcshn->bclsh', ch, bh) * decay
    y_diag = jnp.einsum('bclsh,bcshp->bclhp', scores, xdt)
    decay_to_end = jnp.exp(a_cs[:, :, -1:, :] - a_cs)
    states = jnp.einsum('bclhn,bclh,bclhp->bchpn', bh, decay_to_end, xdt)
    chunk_decay = jnp.exp(a_cs[:, :, -1, :])

    def step(h_prev, inp):
        st, dec = inp
        return h_prev * dec[:, :, None, None] + st, h_prev

    h_final, h_in = lax.scan(step, h0.astype(f32),
                             (jnp.moveaxis(states, 1, 0), jnp.moveaxis(chunk_decay, 1, 0)))
    h_in = jnp.moveaxis(h_in, 0, 1)
    y_off = jnp.einsum('bclhn,bchpn,bclh->bclhp', ch, h_in, jnp.exp(a_cs))
    return (y_diag + y_off).reshape(b, l, h, p), h_final


def _ssd_bidir(xs, dt, a, bm, cm, h0_f, h0_b):
    flip = lambda t: jnp.flip(t, axis=1)
    y_f, h_f = _ssd_scan(xs, dt[:, :, 0, :], a[0], bm, cm, h0_f)
    y_b, h_b = _ssd_scan(flip(xs), flip(dt[:, :, 1, :]), a[1], flip(bm), flip(cm), h0_b)
    return y_f + flip(y_b), h_f, h_b


def _mixer_inputs(h, w_in, conv_w, conv_b, dt_bias):
    b, l = h.shape[0], h.shape[1]
    proj = h @ w_in
    i0 = SSD_DIM
    i1 = i0 + XBC_DIM
    i2 = i1 + 2 * SSD_HEADS
    z, xbc, dt_raw, cm_in = proj[..., :i0], proj[..., i0:i1], proj[..., i1:i2], proj[..., i2:]
    xbc = jax.nn.silu(_dwconv(xbc, conv_w, conv_b))
    gn = SSD_GROUPS * SSD_STATE
    xs = xbc[..., :SSD_DIM].reshape(b, l, SSD_HEADS, SSD_HEAD_DIM)
    bm = xbc[..., SSD_DIM:SSD_DIM + gn].reshape(b, l, SSD_GROUPS, SSD_STATE)
    cm = xbc[..., SSD_DIM + gn:].reshape(b, l, SSD_GROUPS, SSD_STATE)
    dt = jax.nn.softplus(dt_raw.astype(jnp.float32).reshape(b, l, 2, SSD_HEADS)
                         + dt_bias.astype(jnp.float32))
    return z, xs, bm, cm, dt, cm_in


def _ssd_out(y, xs, z, d_skip, norm_g):
    b, l = y.shape[0], y.shape[1]
    y = y + d_skip.astype(jnp.float32)[:, None] * xs.astype(jnp.float32)
    y = y.reshape(b, l, SSD_DIM).astype(z.dtype)
    return _rmsnorm(y * jax.nn.silu(z), norm_g)


def _conv_module(cm_in, dw_w, dw_b, ln_g, ln_b, on_grid):
    a, gate = cm_in[..., :CM_DIM], cm_in[..., CM_DIM:]
    u = a * jax.nn.sigmoid(gate)
    if on_grid:
        b, l, ch = u.shape
        rows = l // GRID_W
        u = _dwconv(u.reshape(b * rows, GRID_W, ch), dw_w, dw_b).reshape(b, l, ch)
    else:
        u = _dwconv(u, dw_w, dw_b)
    return jax.nn.silu(_layernorm(u, ln_g, ln_b))


def _mixer(h_ctx, h_lat, w_in, conv_w, conv_b, dt_bias, a_log, d_skip, ssd_norm_g,
           dw_w, dw_b, ln_g, ln_b, w_out, need_ctx):
    a = -jnp.exp(a_log.astype(jnp.float32))
    zc, xc, bc, cc, dtc, cmc = _mixer_inputs(h_ctx, w_in, conv_w, conv_b, dt_bias)
    zl, xl, bl, cl, dtl, cml = _mixer_inputs(h_lat, w_in, conv_w, conv_b, dt_bias)
    h0 = jnp.zeros((h_ctx.shape[0], SSD_HEADS, SSD_HEAD_DIM, SSD_STATE), jnp.float32)
    yc, hf, hb = _ssd_bidir(xc, dtc, a, bc, cc, h0, h0)
    yl, _, _ = _ssd_bidir(xl, dtl, a, bl, cl, hf, hb)
    lat = jnp.concatenate([_ssd_out(yl, xl, zl, d_skip, ssd_norm_g),
                           _conv_module(cml, dw_w, dw_b, ln_g, ln_b, True)], axis=-1) @ w_out
    if need_ctx:
        ctx_o = jnp.concatenate([_ssd_out(yc, xc, zc, d_skip, ssd_norm_g),
                                 _conv_module(cmc, dw_w, dw_b, ln_g, ln_b, False)], axis=-1) @ w_out
    else:
        ctx_o = None
    return lat, ctx_o


def _swiglu(h, w1, w2):
    hid = h @ w1
    return (jax.nn.silu(hid[..., :FFN_HIDDEN]) * hid[..., FFN_HIDDEN:]) @ w2


def setup_inputs(seed: int = 0) -> dict:
    key = jax.random.key(seed)
    ks = jax.random.split(key, 24)
    nrm = lambda k, shape, s: jax.random.normal(k, shape, jnp.float32) * s
    dt0 = jnp.exp(jax.random.uniform(ks[7], (DEPTH, 2, SSD_HEADS), jnp.float32,
                                     minval=math.log(1e-3), maxval=math.log(1e-1)))
    return {
        'x': nrm(ks[0], (BATCH, SEQ, D_MODEL), 1.0),
        'c': nrm(ks[1], (BATCH, D_MODEL), 1.0),
        'ctx': nrm(ks[2], (BATCH, CTX_LEN, D_MODEL), 1.0),
        'c_ctx': nrm(ks[3], (D_MODEL,), 1.0),
        'w_in': nrm(ks[4], (DEPTH, D_MODEL, IN_DIM), D_MODEL ** -0.5),
        'ssd_conv_w': nrm(ks[5], (DEPTH, SSD_CONV_W, XBC_DIM), SSD_CONV_W ** -0.5),
        'ssd_conv_b': nrm(ks[6], (DEPTH, XBC_DIM), 0.02),
        'dt_bias': dt0 + jnp.log(-jnp.expm1(-dt0)),
        'a_log': jnp.log(jax.random.uniform(ks[8], (DEPTH, 2, SSD_HEADS), jnp.float32,
                                            minval=1.0, maxval=16.0)),
        'd_skip': 1.0 + nrm(ks[9], (DEPTH, SSD_HEADS), 0.1),
        'ssd_norm_g': 1.0 + nrm(ks[10], (DEPTH, SSD_DIM), 0.1),
        'cm_dw_w': nrm(ks[11], (DEPTH, CM_KERNEL, CM_DIM), CM_KERNEL ** -0.5),
        'cm_dw_b': nrm(ks[12], (DEPTH, CM_DIM), 0.02),
        'cm_ln_g': 1.0 + nrm(ks[13], (DEPTH, CM_DIM), 0.1),
        'cm_ln_b': nrm(ks[14], (DEPTH, CM_DIM), 0.02),
        'w_out': nrm(ks[15], (DEPTH, MIX_DIM, D_MODEL), MIX_DIM ** -0.5),
        'w_ffn_in': nrm(ks[16], (DEPTH, D_MODEL, 2 * FFN_HIDDEN), D_MODEL ** -0.5),
        'w_ffn_out': nrm(ks[17], (DEPTH, FFN_HIDDEN, D_MODEL), FFN_HIDDEN ** -0.5),
        'ada_w': nrm(ks[18], (DEPTH, D_MODEL, 6 * D_MODEL), 0.02),
        'ada_b': nrm(ks[19], (DEPTH, 6 * D_MODEL), 0.02),
        'norm1_g': 1.0 + nrm(ks[20], (DEPTH, D_MODEL), 0.1),
        'norm2_g': 1.0 + nrm(ks[21], (DEPTH, D_MODEL), 0.1),
        'final_norm_g': 1.0 + nrm(ks[22], (D_MODEL,), 0.1),
    }


def reference(x, c, ctx, c_ctx, w_in, ssd_conv_w, ssd_conv_b, dt_bias, a_log, d_skip,
              ssd_norm_g, cm_dw_w, cm_dw_b, cm_ln_g, cm_ln_b, w_out, w_ffn_in, w_ffn_out,
              ada_w, ada_b, norm1_g, norm2_g, final_norm_g):
    sc = jax.nn.silu(c)
    sc_ctx = jax.nn.silu(c_ctx)
    for i in range(DEPTH):
        need_ctx = i < DEPTH - 1
        mod_l = jnp.split((sc @ ada_w[i] + ada_b[i])[:, None, :], 6, axis=-1)
        mod_c = jnp.split(sc_ctx @ ada_w[i] + ada_b[i], 6, axis=-1)
        sh1, s1, g1, sh2, s2, g2 = mod_l
        csh1, cs1, cg1, csh2, cs2, cg2 = mod_c
        h_lat = _rmsnorm(x, norm1_g[i]) * (1.0 + s1) + sh1
        h_ctx = _rmsnorm(ctx, norm1_g[i]) * (1.0 + cs1) + csh1
        mix_lat, mix_ctx = _mixer(h_ctx, h_lat, w_in[i], ssd_conv_w[i], ssd_conv_b[i],
                                  dt_bias[i], a_log[i], d_skip[i], ssd_norm_g[i],
                                  cm_dw_w[i], cm_dw_b[i], cm_ln_g[i], cm_ln_b[i], w_out[i],
                                  need_ctx)
        x = x + g1 * mix_lat
        x = x + g2 * _swiglu(_rmsnorm(x, norm2_g[i]) * (1.0 + s2) + sh2, w_ffn_in[i], w_ffn_out[i])
        if need_ctx:
            ctx = ctx + cg1 * mix_ctx
            ctx = ctx + cg2 * _swiglu(_rmsnorm(ctx, norm2_g[i]) * (1.0 + cs2) + csh2,
                                      w_ffn_in[i], w_ffn_out[i])
    return _rmsnorm(x, final_norm_g)
```

```python
import functools

import jax
import jax.numpy as jnp
from jax import lax
from jax.experimental import pallas as pl
from jax.experimental.pallas import tpu as pltpu

F32 = jnp.float32
BF16 = jnp.bfloat16

D_MODEL = 1024
SSD_HEADS = 16
SSD_HEAD_DIM = 64
SSD_DIM = SSD_HEADS * SSD_HEAD_DIM
SSD_GROUPS = 2
SSD_STATE = 128
SSD_CONV_W = 5
CHUNK = 128
XBC_DIM = SSD_DIM + 2 * SSD_GROUPS * SSD_STATE
CM_DIM = D_MODEL
CM_KERNEL = 31
GRID_W = 64
FFN_HIDDEN = 2816
EPS = 1e-6

LANES = 128
DT_PAD = LANES
IN_COLS = SSD_DIM + XBC_DIM + 2 * CM_DIM + DT_PAD
HALO = 16
VMEM_LIMIT = 56 * 1024 * 1024

TM_IN = 512
TM_OUT = 256
TR_CM = 256


def _silu(v):
    return v * jax.nn.sigmoid(v)


def _dot(a, b):
    return jnp.dot(a, b, preferred_element_type=F32)


def _mods_kernel(c_ref, w_ref, b_ref, o_ref):
    sc = _silu(c_ref[...])
    o_ref[...] = _dot(sc.astype(BF16), w_ref[...].astype(BF16)) + b_ref[...]


def _mods(c_all, ada_w, ada_b):
    depth = ada_w.shape[0]
    nrow = c_all.shape[0]
    tn = 1536
    return pl.pallas_call(
        _mods_kernel,
        out_shape=jax.ShapeDtypeStruct((depth, nrow, 6 * D_MODEL), F32),
        grid=(depth, 6 * D_MODEL // tn),
        in_specs=[
            pl.BlockSpec((nrow, D_MODEL), lambda i, j: (0, 0)),
            pl.BlockSpec((None, D_MODEL, tn), lambda i, j: (i, 0, j)),
            pl.BlockSpec((None, 1, tn), lambda i, j: (i, 0, j)),
        ],
        out_specs=pl.BlockSpec((None, nrow, tn), lambda i, j: (i, 0, j)),
        compiler_params=pltpu.CompilerParams(vmem_limit_bytes=VMEM_LIMIT),
        name="adaln_mods",
    )(c_all, ada_w, ada_b.reshape(depth, 1, 6 * D_MODEL))


def _inproj_kernel(x_ref, mod_ref, g_ref, w_ref, z_ref, xbc_ref, cm_ref, dt_ref):
    x = x_ref[...]
    ms = jnp.mean(x * x, axis=-1, keepdims=True)
    y = x * lax.rsqrt(ms + EPS) * g_ref[...]
    sh1 = mod_ref[:, 0:D_MODEL]
    s1 = mod_ref[:, D_MODEL:2 * D_MODEL]
    h = (y * (1.0 + s1) + sh1).astype(BF16)
    step = 512
    off = 0
    for ref, width in ((z_ref, SSD_DIM), (xbc_ref, XBC_DIM), (cm_ref, 2 * CM_DIM)):
        for c0 in range(0, width, step):
            ref[:, c0:c0 + step] = _dot(h, w_ref[:, off + c0:off + c0 + step]).astype(ref.dtype)
        off += width
    dt_ref[...] = _dot(h, w_ref[:, off:off + DT_PAD])


def _inproj(xall, mods, norm_g, w_in_b, layer, n_lat_rows, lat_len):
    rows = xall.shape[0]
    tm = TM_IN
    n_lat_tiles = n_lat_rows // tm
    tiles_per_batch = lat_len // tm
    n_batch = n_lat_rows // lat_len

    def mod_idx(t):
        return jnp.where(t < n_lat_tiles, t // tiles_per_batch, n_batch)

    return pl.pallas_call(
        _inproj_kernel,
        out_shape=(
            jax.ShapeDtypeStruct((rows, SSD_DIM), BF16),
            jax.ShapeDtypeStruct((rows, XBC_DIM), BF16),
            jax.ShapeDtypeStruct((rows, 2 * CM_DIM), BF16),
            jax.ShapeDtypeStruct((rows, DT_PAD), F32),
        ),
        grid=(rows // tm,),
        in_specs=[
            pl.BlockSpec((tm, D_MODEL), lambda t: (t, 0)),
            pl.BlockSpec((None, None, 1, 6 * D_MODEL), lambda t: (layer, mod_idx(t), 0, 0)),
            pl.BlockSpec((None, 1, D_MODEL), lambda t: (layer, 0, 0)),
            pl.BlockSpec((None, D_MODEL, IN_COLS), lambda t: (layer, 0, 0)),
        ],
        out_specs=(
            pl.BlockSpec((tm, SSD_DIM), lambda t: (t, 0)),
            pl.BlockSpec((tm, XBC_DIM), lambda t: (t, 0)),
            pl.BlockSpec((tm, 2 * CM_DIM), lambda t: (t, 0)),
            pl.BlockSpec((tm, DT_PAD), lambda t: (t, 0)),
        ),
        compiler_params=pltpu.CompilerParams(vmem_limit_bytes=VMEM_LIMIT),
        name="inproj",
    )(xall, mods, norm_g, w_in_b)


def _split_hi_mid_lo(v):
    hi = v.astype(BF16)
    r = v - hi.astype(F32)
    mid = r.astype(BF16)
    lo = (r - mid.astype(F32)).astype(BF16)
    return hi, mid, lo


def _ssd_kernel(xf_ref, xfp_ref, xfn_ref, xb_ref, xbp_ref, xbn_ref, dtf_ref, dtb_ref,
                cw_ref, cb_ref, dtbias_ref, alog_ref, dskip_ref, exp_ref,
                yf_ref, yb_ref, ext_ref, act_ref, sf_ref, sb_ref, *, n_ctx, n_lat):
    j = pl.program_id(1)

    @pl.when(j == 0)
    def _():
        sf_ref[...] = jnp.zeros_like(sf_ref)
        sb_ref[...] = jnp.zeros_like(sb_ref)

    is_ctx = j < n_ctx
    nchunks = jnp.where(is_ctx, n_ctx, n_lat)
    cf = jnp.where(is_ctx, j, j - n_ctx)
    cbk = nchunks - 1 - cf

    ri = lax.broadcasted_iota(jnp.int32, (CHUNK, CHUNK), 0)
    ci = lax.broadcasted_iota(jnp.int32, (CHUNK, CHUNK), 1)
    low_mask = ri >= ci
    up_mask = ri <= ci
    lane_lo = ci < SSD_HEAD_DIM
    a_neg = -jnp.exp(alog_ref[...])

    def conv_act(x_ref, xp_ref, xn_ref, cidx):
        prev = xp_ref[...].astype(F32)[HALO - 8:HALO]
        nxt = xn_ref[...].astype(F32)[0:8]
        ext_ref[0:8, :] = jnp.where(cidx > 0, prev, 0.0)
        ext_ref[8:8 + CHUNK, :] = x_ref[...].astype(F32)
        ext_ref[8 + CHUNK:16 + CHUNK, :] = jnp.where(cidx < nchunks - 1, nxt, 0.0)
        for lb in range(XBC_DIM // LANES):
            cols = slice(lb * LANES, (lb + 1) * LANES)
            acc = jnp.broadcast_to(cb_ref[:, cols], (CHUNK, LANES))
            for k in range(SSD_CONV_W):
                acc = acc + cw_ref[k:k + 1, cols] * ext_ref[6 + k:6 + k + CHUNK, cols]
            act_ref[:, cols] = _silu(acc)

    def direction(dt_ref, col_off, forward, s_ref, y_ref):
        dtv = jax.nn.softplus(dt_ref[...] + dtbias_ref[...])
        dta = dtv * a_neg
        tri = jnp.where(low_mask if forward else up_mask, 1.0, 0.0).astype(BF16)
        hi, mid, lo = _split_hi_mid_lo(dta)
        acs = _dot(tri, hi) + _dot(tri, mid) + _dot(tri, lo)
        acs_row = acs.T
        dt_row = dtv.T
        edge = acs[CHUNK - 1:CHUNK, :] if forward else acs[0:1, :]
        wst = dtv * jnp.exp(edge - acs)
        cd16 = jnp.broadcast_to(jnp.exp(edge), (16, LANES))
        v = jnp.concatenate([wst, cd16], axis=0)
        v_hi = v.astype(BF16)
        v_mid = (v - v_hi.astype(F32)).astype(BF16)
        expanded = _dot(jnp.concatenate([v_hi, v_mid], axis=1),
                        exp_ref[0 if forward else 1])
        w_exp = expanded[0:CHUNK]
        cd_exp = expanded[CHUNK:CHUNK + 1]

        mask = low_mask if forward else up_mask
        for g in range(SSD_GROUPS):
            b_g = act_ref[:, SSD_DIM + g * SSD_STATE:SSD_DIM + (g + 1) * SSD_STATE].astype(BF16)
            c_off = SSD_DIM + SSD_GROUPS * SSD_STATE + g * SSD_STATE
            c_f32 = act_ref[:, c_off:c_off + SSD_STATE]
            gmat = lax.dot_general(c_f32.astype(BF16), b_g, (((1,), (1,)), ((), ())),
                                   preferred_element_type=F32)
            heads_per_group = SSD_HEADS // SSD_GROUPS
            for pair in range(heads_per_group // 2):
                h0 = g * heads_per_group + 2 * pair
                cols = slice(h0 * SSD_HEAD_DIM, (h0 + 2) * SSD_HEAD_DIM)
                xs_pair = act_ref[:, cols]
                rhs = jnp.concatenate([xs_pair.astype(BF16), s_ref[:, cols].astype(BF16)], axis=0)
                res = []
                for h in (h0, h0 + 1):
                    c = col_off + h
                    col = jnp.broadcast_to(acs[:, c:c + 1], (CHUNK, CHUNK))
                    seg = col - acs_row[c:c + 1, :]
                    dec = jnp.exp(jnp.where(mask, seg, -jnp.inf)) * dt_row[c:c + 1, :]
                    sc = (gmat * dec).astype(BF16)
                    ce = (c_f32 * jnp.exp(col)).astype(BF16)
                    res.append(_dot(jnp.concatenate([sc, ce], axis=1), rhs))
                y = jnp.where(lane_lo, res[0], res[1])
                if forward:
                    y = y + dskip_ref[:, cols] * xs_pair
                y_ref[:, cols] = y
            gcols = slice(g * heads_per_group * SSD_HEAD_DIM, (g + 1) * heads_per_group * SSD_HEAD_DIM)
            xw = (act_ref[:, gcols] * w_exp[:, gcols]).astype(BF16)
            upd = lax.dot_general(b_g, xw, (((0,), (0,)), ((), ())), preferred_element_type=F32)
            s_ref[:, gcols] = s_ref[:, gcols] * cd_exp[:, gcols] + upd

    conv_act(xf_ref, xfp_ref, xfn_ref, cf)
    direction(dtf_ref, 0, True, sf_ref, yf_ref)
    conv_act(xb_ref, xbp_ref, xbn_ref, cbk)
    direction(dtb_ref, SSD_HEADS, False, sb_ref, yb_ref)


def _ssd(xbc, dt, conv_w, conv_b, dt_bias, a_log, d_skip_exp, expand, layer, n_batch, lat_len, ctx_len):
    rows = xbc.shape[0]
    n_lat = lat_len // CHUNK
    n_ctx = ctx_len // CHUNK
    ctx0 = n_batch * n_lat
    last16 = rows // HALO - 1
    per16 = CHUNK // HALO

    def fwd_blk(b, j):
        return jnp.where(j < n_ctx, ctx0 + b * n_ctx + j, b * n_lat + (j - n_ctx))

    def bwd_blk(b, j):
        return jnp.where(j < n_ctx, ctx0 + b * n_ctx + (n_ctx - 1 - j),
                         b * n_lat + (n_lat - 1 - (j - n_ctx)))

    def main(blk):
        return lambda b, j: (blk(b, j), 0)

    def prev(blk):
        return lambda b, j: (jnp.maximum(blk(b, j) * per16 - 1, 0), 0)

    def nxt(blk):
        return lambda b, j: (jnp.minimum(blk(b, j) * per16 + per16, last16), 0)

    const3 = lambda b, j: (layer, 0, 0)
    x_specs = []
    for blk in (fwd_blk, bwd_blk):
        x_specs += [pl.BlockSpec((CHUNK, XBC_DIM), main(blk)),
                    pl.BlockSpec((HALO, XBC_DIM), prev(blk)),
                    pl.BlockSpec((HALO, XBC_DIM), nxt(blk))]
    kern = functools.partial(_ssd_kernel, n_ctx=n_ctx, n_lat=n_lat)
    return pl.pallas_call(
        kern,
        out_shape=(jax.ShapeDtypeStruct((rows, SSD_DIM), F32),
                   jax.ShapeDtypeStruct((rows, SSD_DIM), F32)),
        grid=(n_batch, n_ctx + n_lat),
        in_specs=x_specs + [
            pl.BlockSpec((CHUNK, DT_PAD), main(fwd_blk)),
            pl.BlockSpec((CHUNK, DT_PAD), main(bwd_blk)),
            pl.BlockSpec((None, SSD_CONV_W, XBC_DIM), const3),
            pl.BlockSpec((None, 1, XBC_DIM), const3),
            pl.BlockSpec((None, 1, DT_PAD), const3),
            pl.BlockSpec((None, 1, DT_PAD), const3),
            pl.BlockSpec((None, 1, SSD_DIM), const3),
            pl.BlockSpec((2, 2 * LANES, SSD_DIM), lambda b, j: (0, 0, 0)),
        ],
        out_specs=(pl.BlockSpec((CHUNK, SSD_DIM), main(fwd_blk)),
                   pl.BlockSpec((CHUNK, SSD_DIM), main(bwd_blk))),
        scratch_shapes=[
            pltpu.VMEM((CHUNK + 16, XBC_DIM), F32),
            pltpu.VMEM((CHUNK, XBC_DIM), F32),
            pltpu.VMEM((SSD_STATE, SSD_DIM), F32),
            pltpu.VMEM((SSD_STATE, SSD_DIM), F32),
        ],
        compiler_params=pltpu.CompilerParams(
            dimension_semantics=("arbitrary", "arbitrary"), vmem_limit_bytes=VMEM_LIMIT),
        name="ssd_scan",
    )(xbc, xbc, xbc, xbc, xbc, xbc, dt, dt, conv_w, conv_b, dt_bias, a_log, d_skip_exp, expand)


def _cm_kernel(cm_ref, w_ref, b_ref, g_ref, beta_ref, o_ref, pad_ref, conv_ref, *,
               n_lat_tiles, ctx_len):
    t = pl.program_id(0)
    tr = cm_ref.shape[0]
    half = CM_KERNEL // 2
    pad = 16

    def run(seg_len):
        nseg = tr // seg_len
        stride = seg_len + 2 * pad
        zeros = jnp.zeros((pad, CM_DIM), F32)
        for s in range(nseg):
            rows = slice(s * seg_len, (s + 1) * seg_len)
            a = cm_ref[rows, 0:CM_DIM].astype(F32)
            gate = cm_ref[rows, CM_DIM:2 * CM_DIM].astype(F32)
            base = s * stride
            pad_ref[base:base + pad, :] = zeros
            pad_ref[base + pad:base + pad + seg_len, :] = a * jax.nn.sigmoid(gate)
            pad_ref[base + pad + seg_len:base + stride, :] = zeros
        for s in range(nseg):
            base = s * stride + pad - half
            for lb in range(CM_DIM // LANES):
                cols = slice(lb * LANES, (lb + 1) * LANES)
                acc = jnp.broadcast_to(b_ref[:, cols], (seg_len, LANES))
                for k in range(CM_KERNEL):
                    acc = acc + w_ref[k:k + 1, cols] * pad_ref[base + k:base + k + seg_len, cols]
                conv_ref[s * seg_len:(s + 1) * seg_len, cols] = acc
        v = conv_ref[...]
        mu = jnp.mean(v, axis=-1, keepdims=True)
        xc = v - mu
        var = jnp.mean(xc * xc, axis=-1, keepdims=True)
        y = xc * lax.rsqrt(var + EPS) * g_ref[...] + beta_ref[...]
        o_ref[...] = _silu(y).astype(o_ref.dtype)

    @pl.when(t < n_lat_tiles)
    def _():
        run(GRID_W)

    @pl.when(t >= n_lat_tiles)
    def _():
        run(ctx_len)


def _conv_module(cm, dw_w, dw_b, ln_g, ln_b, layer, n_lat_rows, ctx_len, final):
    rows = n_lat_rows if final else cm.shape[0]
    tr = TR_CM
    assert ctx_len == tr and tr % GRID_W == 0
    const3 = lambda t: (layer, 0, 0)
    kern = functools.partial(_cm_kernel, n_lat_tiles=n_lat_rows // tr, ctx_len=ctx_len)
    pad_rows = max((tr // GRID_W) * (GRID_W + 32), ctx_len + 32)
    return pl.pallas_call(
        kern,
        out_shape=jax.ShapeDtypeStruct((rows, CM_DIM), BF16),
        grid=(rows // tr,),
        in_specs=[
            pl.BlockSpec((tr, 2 * CM_DIM), lambda t: (t, 0)),
            pl.BlockSpec((None, CM_KERNEL, CM_DIM), const3),
            pl.BlockSpec((None, 1, CM_DIM), const3),
            pl.BlockSpec((None, 1, CM_DIM), const3),
            pl.BlockSpec((None, 1, CM_DIM), const3),
        ],
        out_specs=pl.BlockSpec((tr, CM_DIM), lambda t: (t, 0)),
        scratch_shapes=[pltpu.VMEM((pad_rows, CM_DIM), F32), pltpu.VMEM((tr, CM_DIM), F32)],
        compiler_params=pltpu.CompilerParams(vmem_limit_bytes=VMEM_LIMIT),
        name="conv_module",
    )(cm, dw_w, dw_b, ln_g, ln_b)


def _outffn_kernel(x_ref, yf_ref, yb_ref, z_ref, cmo_ref, mod_ref, ng_ref, n2g_ref, fg_ref,
                   wo_ref, w1_ref, w2_ref, o_ref, act_ref, *, final):
    mod = lambda i: mod_ref[:, i * D_MODEL:(i + 1) * D_MODEL]
    v = (yf_ref[...] + yb_ref[...]) * _silu(z_ref[...].astype(F32))
    ms = jnp.mean(v * v, axis=-1, keepdims=True)
    so = (v * lax.rsqrt(ms + EPS) * ng_ref[...]).astype(BF16)
    mix = _dot(so, wo_ref[0:SSD_DIM, :]) + _dot(cmo_ref[...], wo_ref[SSD_DIM:SSD_DIM + CM_DIM, :])
    x1 = x_ref[...] + mod(2) * mix
    ms2 = jnp.mean(x1 * x1, axis=-1, keepdims=True)
    h2 = (x1 * lax.rsqrt(ms2 + EPS) * n2g_ref[...] * (1.0 + mod(4)) + mod(3)).astype(BF16)
    step = 256
    for c0 in range(0, FFN_HIDDEN, step):
        a = _dot(h2, w1_ref[:, c0:c0 + step])
        b = _dot(h2, w1_ref[:, FFN_HIDDEN + c0:FFN_HIDDEN + c0 + step])
        act_ref[:, c0:c0 + step] = (_silu(a) * b).astype(BF16)
    x2 = x1 + mod(5) * _dot(act_ref[...], w2_ref[...])
    if final:
        msf = jnp.mean(x2 * x2, axis=-1, keepdims=True)
        x2 = x2 * lax.rsqrt(msf + EPS) * fg_ref[...]
    o_ref[...] = x2


def _outffn(xall, yf, yb, z, cmo, mods, ssd_norm_g, norm2_g, final_g, w_out_b, w1_b, w2_b,
            layer, n_lat_rows, lat_len, final):
    tm = TM_OUT
    rows = n_lat_rows if final else xall.shape[0]
    n_lat_tiles = n_lat_rows // tm
    tiles_per_batch = lat_len // tm
    n_batch = n_lat_rows // lat_len

    def mod_idx(t):
        return jnp.where(t < n_lat_tiles, t // tiles_per_batch, n_batch)

    row_spec = lambda width: pl.BlockSpec((tm, width), lambda t: (t, 0))
    const3 = lambda t: (layer, 0, 0)
    single = pl.Buffered(1)
    kern = functools.partial(_outffn_kernel, final=final)
    return pl.pallas_call(
        kern,
        out_shape=jax.ShapeDtypeStruct((rows, D_MODEL), F32),
        grid=(rows // tm,),
        in_specs=[
            row_spec(D_MODEL), row_spec(SSD_DIM), row_spec(SSD_DIM), row_spec(SSD_DIM),
            row_spec(CM_DIM),
            pl.BlockSpec((None, None, 1, 6 * D_MODEL), lambda t: (layer, mod_idx(t), 0, 0)),
            pl.BlockSpec((None, 1, SSD_DIM), const3),
            pl.BlockSpec((None, 1, D_MODEL), const3),
            pl.BlockSpec((1, D_MODEL), lambda t: (0, 0)),
            pl.BlockSpec((None, SSD_DIM + CM_DIM, D_MODEL), const3, pipeline_mode=single),
            pl.BlockSpec((None, D_MODEL, 2 * FFN_HIDDEN), const3, pipeline_mode=single),
            pl.BlockSpec((None, FFN_HIDDEN, D_MODEL), const3, pipeline_mode=single),
        ],
        out_specs=row_spec(D_MODEL),
        scratch_shapes=[pltpu.VMEM((tm, FFN_HIDDEN), BF16)],
        compiler_params=pltpu.CompilerParams(vmem_limit_bytes=VMEM_LIMIT),
        name="outproj_ffn",
    )(xall, yf, yb, z, cmo, mods, ssd_norm_g, norm2_g, final_g, w_out_b, w1_b, w2_b)


def kernel(x, c, ctx, c_ctx, w_in, ssd_conv_w, ssd_conv_b, dt_bias, a_log, d_skip, ssd_norm_g,
           cm_dw_w, cm_dw_b, cm_ln_g, cm_ln_b, w_out, w_ffn_in, w_ffn_out, ada_w, ada_b,
           norm1_g, norm2_g, final_norm_g):
    n_batch, lat_len, _ = x.shape
    ctx_len = ctx.shape[1]
    depth = w_in.shape[0]
    n_lat_rows = n_batch * lat_len
    assert lat_len % TM_IN == 0 and (n_batch * ctx_len) % TM_IN == 0
    assert lat_len % CHUNK == 0 and ctx_len % CHUNK == 0 and n_batch + 1 <= 16

    xall = jnp.concatenate([x.reshape(n_lat_rows, D_MODEL),
                            ctx.reshape(n_batch * ctx_len, D_MODEL)], axis=0)

    c_all = jnp.concatenate([c, c_ctx[None, :],
                             jnp.zeros((16 - n_batch - 1, D_MODEL), F32)], axis=0)
    mods = _mods(c_all, ada_w, ada_b).reshape(depth, 16, 1, 6 * D_MODEL)

    i0, i1, i2 = SSD_DIM, SSD_DIM + XBC_DIM, SSD_DIM + XBC_DIM + 2 * SSD_HEADS
    w_in_b = jnp.concatenate(
        [w_in[..., :i1], w_in[..., i2:], w_in[..., i1:i2],
         jnp.zeros((depth, D_MODEL, DT_PAD - 2 * SSD_HEADS), F32)], axis=-1).astype(BF16)
    w_out_b = w_out.astype(BF16)
    w1_b = w_ffn_in.astype(BF16)
    w2_b = w_ffn_out.astype(BF16)
    pad_lanes = lambda v: jnp.pad(v.reshape(depth, 1, 2 * SSD_HEADS),
                                  ((0, 0), (0, 0), (0, DT_PAD - 2 * SSD_HEADS)))
    dt_bias_p = pad_lanes(dt_bias)
    a_log_p = pad_lanes(a_log)
    d_skip_exp = jnp.repeat(d_skip, SSD_HEAD_DIM, axis=-1).reshape(depth, 1, SSD_DIM)
    row3 = lambda v: v.reshape(depth, 1, v.shape[-1])

    r = jnp.arange(2 * LANES)[:, None] % LANES
    col_head = jnp.arange(SSD_DIM)[None, :] // SSD_HEAD_DIM
    expand = jnp.stack([(r == col_head), (r == col_head + SSD_HEADS)]).astype(BF16)

    for i in range(depth):
        final = i == depth - 1
        z, xbc, cm, dt = _inproj(xall, mods, row3(norm1_g), w_in_b, i, n_lat_rows, lat_len)
        yf, yb = _ssd(xbc, dt, ssd_conv_w, row3(ssd_conv_b), dt_bias_p, a_log_p, d_skip_exp,
                      expand, i, n_batch, lat_len, ctx_len)
        cmo = _conv_module(cm, cm_dw_w, row3(cm_dw_b), row3(cm_ln_g), row3(cm_ln_b), i,
                           n_lat_rows, ctx_len, final)
        xall = _outffn(xall, yf, yb, z, cmo, mods, row3(ssd_norm_g), row3(norm2_g),
                       final_norm_g.reshape(1, D_MODEL), w_out_b, w1_b, w2_b,
                       i, n_lat_rows, lat_len, final)
    return xall.reshape(n_batch, lat_len, D_MODEL)
```

```python
import functools

import jax
import jax.numpy as jnp
from jax import lax
from jax.experimental import pallas as pl
from jax.experimental.pallas import tpu as pltpu

F32 = jnp.float32
BF16 = jnp.bfloat16

D_MODEL = 1024
SSD_HEADS = 16
SSD_HEAD_DIM = 64
SSD_DIM = SSD_HEADS * SSD_HEAD_DIM
SSD_GROUPS = 2
SSD_STATE = 128
SSD_CONV_W = 5
CHUNK = 128
XBC_DIM = SSD_DIM + 2 * SSD_GROUPS * SSD_STATE
CM_DIM = D_MODEL
CM_KERNEL = 31
GRID_W = 64
FFN_HIDDEN = 2816
EPS = 1e-6

LANES = 128
SUBLANES = 8
DT_PAD = LANES
IN_COLS = SSD_DIM + XBC_DIM + 2 * CM_DIM + DT_PAD
HALO = 16
VMEM_LIMIT = 56 * 1024 * 1024

TM_IN = 512
TM_OUT = 256
TR_CM = 256
CM_BLK_ROWS = 4
CM_BLK_TILES = 4
CM_PAD = 16
CM_TAP_ROWS = 32


def _silu(v):
    return v * jax.nn.sigmoid(v)


def _dot(a, b):
    return jnp.dot(a, b, preferred_element_type=F32)


def _mods_kernel(c_ref, w_ref, b_ref, o_ref):
    sc = _silu(c_ref[...])
    o_ref[...] = _dot(sc.astype(BF16), w_ref[...].astype(BF16)) + b_ref[...]


def _mods(c_all, ada_w, ada_b):
    depth = ada_w.shape[0]
    nrow = c_all.shape[0]
    tn = 1536
    return pl.pallas_call(
        _mods_kernel,
        out_shape=jax.ShapeDtypeStruct((depth, nrow, 6 * D_MODEL), F32),
        grid=(depth, 6 * D_MODEL // tn),
        in_specs=[
            pl.BlockSpec((nrow, D_MODEL), lambda i, j: (0, 0)),
            pl.BlockSpec((None, D_MODEL, tn), lambda i, j: (i, 0, j)),
            pl.BlockSpec((None, 1, tn), lambda i, j: (i, 0, j)),
        ],
        out_specs=pl.BlockSpec((None, nrow, tn), lambda i, j: (i, 0, j)),
        compiler_params=pltpu.CompilerParams(vmem_limit_bytes=VMEM_LIMIT),
        name="adaln_mods",
    )(c_all, ada_w, ada_b.reshape(depth, 1, 6 * D_MODEL))


def _inproj_kernel(x_ref, mod_ref, g_ref, w_ref, z_ref, xbc_ref, cm_ref, dt_ref):
    x = x_ref[...]
    ms = jnp.mean(x * x, axis=-1, keepdims=True)
    y = x * lax.rsqrt(ms + EPS) * g_ref[...]
    sh1 = mod_ref[:, 0:D_MODEL]
    s1 = mod_ref[:, D_MODEL:2 * D_MODEL]
    h = (y * (1.0 + s1) + sh1).astype(BF16)
    step = 512
    off = 0
    for ref, width in ((z_ref, SSD_DIM), (xbc_ref, XBC_DIM), (cm_ref, 2 * CM_DIM)):
        for c0 in range(0, width, step):
            ref[:, c0:c0 + step] = _dot(h, w_ref[:, off + c0:off + c0 + step]).astype(ref.dtype)
        off += width
    dt_ref[...] = _dot(h, w_ref[:, off:off + DT_PAD])


def _inproj(xall, mods, norm_g, w_in_b, layer, n_lat_rows, lat_len):
    rows = xall.shape[0]
    tm = TM_IN
    n_lat_tiles = n_lat_rows // tm
    tiles_per_batch = lat_len // tm
    n_batch = n_lat_rows // lat_len

    def mod_idx(t):
        return jnp.where(t < n_lat_tiles, t // tiles_per_batch, n_batch)

    return pl.pallas_call(
        _inproj_kernel,
        out_shape=(
            jax.ShapeDtypeStruct((rows, SSD_DIM), BF16),
            jax.ShapeDtypeStruct((rows, XBC_DIM), BF16),
            jax.ShapeDtypeStruct((rows, 2 * CM_DIM), BF16),
            jax.ShapeDtypeStruct((rows, DT_PAD), F32),
        ),
        grid=(rows // tm,),
        in_specs=[
            pl.BlockSpec((tm, D_MODEL), lambda t: (t, 0)),
            pl.BlockSpec((None, None, 1, 6 * D_MODEL), lambda t: (layer, mod_idx(t), 0, 0)),
            pl.BlockSpec((None, 1, D_MODEL), lambda t: (layer, 0, 0)),
            pl.BlockSpec((None, D_MODEL, IN_COLS), lambda t: (layer, 0, 0)),
        ],
        out_specs=(
            pl.BlockSpec((tm, SSD_DIM), lambda t: (t, 0)),
            pl.BlockSpec((tm, XBC_DIM), lambda t: (t, 0)),
            pl.BlockSpec((tm, 2 * CM_DIM), lambda t: (t, 0)),
            pl.BlockSpec((tm, DT_PAD), lambda t: (t, 0)),
        ),
        compiler_params=pltpu.CompilerParams(vmem_limit_bytes=VMEM_LIMIT),
        name="inproj",
    )(xall, mods, norm_g, w_in_b)


def _split_hi_mid_lo(v):
    hi = v.astype(BF16)
    r = v - hi.astype(F32)
    mid = r.astype(BF16)
    lo = (r - mid.astype(F32)).astype(BF16)
    return hi, mid, lo


def _ssd_kernel(xf_ref, xfp_ref, xfn_ref, xb_ref, xbp_ref, xbn_ref, dtf_ref, dtb_ref,
                cw_ref, cb_ref, dtbias_ref, alog_ref, dskip_ref, exp_ref,
                yf_ref, yb_ref, ext_ref, cache_ref, sf_ref, sb_ref, *, n_ctx, n_lat):
    j = pl.program_id(1)

    @pl.when(j == 0)
    def _():
        sf_ref[...] = jnp.zeros_like(sf_ref)
        sb_ref[...] = jnp.zeros_like(sb_ref)

    is_ctx = j < n_ctx
    nchunks = jnp.where(is_ctx, n_ctx, n_lat)
    cf = jnp.where(is_ctx, j, j - n_ctx)
    cbk = nchunks - 1 - cf
    seq_base = jnp.where(is_ctx, 0, n_ctx)
    slot_f = seq_base + cf
    slot_b = seq_base + cbk
    fwd_first = 2 * cf <= nchunks - 1
    bwd_first = 2 * cf < nchunks - 1

    ri = lax.broadcasted_iota(jnp.int32, (CHUNK, CHUNK), 0)
    ci = lax.broadcasted_iota(jnp.int32, (CHUNK, CHUNK), 1)
    low_mask = ri >= ci
    up_mask = ri <= ci
    lane_lo = ci < SSD_HEAD_DIM
    a_neg = -jnp.exp(alog_ref[...])

    def conv_act(x_ref, xp_ref, xn_ref, cidx, act_ref):
        prev = xp_ref[...].astype(F32)[HALO - 8:HALO]
        nxt = xn_ref[...].astype(F32)[0:8]
        ext_ref[0:8, :] = jnp.where(cidx > 0, prev, 0.0)
        ext_ref[8:8 + CHUNK, :] = x_ref[...].astype(F32)
        ext_ref[8 + CHUNK:16 + CHUNK, :] = jnp.where(cidx < nchunks - 1, nxt, 0.0)
        for lb in range(XBC_DIM // LANES):
            cols = slice(lb * LANES, (lb + 1) * LANES)
            acc = jnp.broadcast_to(cb_ref[:, cols], (CHUNK, LANES))
            for k in range(SSD_CONV_W):
                acc = acc + cw_ref[k:k + 1, cols] * ext_ref[6 + k:6 + k + CHUNK, cols]
            act_ref[:, cols] = _silu(acc)

    def direction(dt_ref, col_off, forward, s_ref, y_ref, act_ref):
        dtv = jax.nn.softplus(dt_ref[...] + dtbias_ref[...])
        dta = dtv * a_neg
        tri = jnp.where(low_mask if forward else up_mask, 1.0, 0.0).astype(BF16)
        hi, mid, lo = _split_hi_mid_lo(dta)
        acs = _dot(tri, hi) + _dot(tri, mid) + _dot(tri, lo)
        acs_row = acs.T
        dt_row = dtv.T
        edge = acs[CHUNK - 1:CHUNK, :] if forward else acs[0:1, :]
        wst = dtv * jnp.exp(edge - acs)
        cd16 = jnp.broadcast_to(jnp.exp(edge), (16, LANES))
        v = jnp.concatenate([wst, cd16], axis=0)
        v_hi = v.astype(BF16)
        v_mid = (v - v_hi.astype(F32)).astype(BF16)
        expanded = _dot(jnp.concatenate([v_hi, v_mid], axis=1),
                        exp_ref[0 if forward else 1])
        w_exp = expanded[0:CHUNK]
        cd_exp = expanded[CHUNK:CHUNK + 1]

        mask = low_mask if forward else up_mask
        for g in range(SSD_GROUPS):
            b_g = act_ref[:, SSD_DIM + g * SSD_STATE:SSD_DIM + (g + 1) * SSD_STATE].astype(BF16)
            c_off = SSD_DIM + SSD_GROUPS * SSD_STATE + g * SSD_STATE
            c_f32 = act_ref[:, c_off:c_off + SSD_STATE]
            gmat = lax.dot_general(c_f32.astype(BF16), b_g, (((1,), (1,)), ((), ())),
                                   preferred_element_type=F32)
            heads_per_group = SSD_HEADS // SSD_GROUPS
            for pair in range(heads_per_group // 2):
                h0 = g * heads_per_group + 2 * pair
                cols = slice(h0 * SSD_HEAD_DIM, (h0 + 2) * SSD_HEAD_DIM)
                xs_pair = act_ref[:, cols]
                rhs = jnp.concatenate([xs_pair.astype(BF16), s_ref[:, cols].astype(BF16)], axis=0)
                res = []
                for h in (h0, h0 + 1):
                    c = col_off + h
                    col = jnp.broadcast_to(acs[:, c:c + 1], (CHUNK, CHUNK))
                    seg = col - acs_row[c:c + 1, :]
                    dec = jnp.exp(jnp.where(mask, seg, -jnp.inf)) * dt_row[c:c + 1, :]
                    sc = (gmat * dec).astype(BF16)
                    ce = (c_f32 * jnp.exp(col)).astype(BF16)
                    res.append(_dot(jnp.concatenate([sc, ce], axis=1), rhs))
                y = jnp.where(lane_lo, res[0], res[1])
                if forward:
                    y = y + dskip_ref[:, cols] * xs_pair
                y_ref[:, cols] = y
            gcols = slice(g * heads_per_group * SSD_HEAD_DIM, (g + 1) * heads_per_group * SSD_HEAD_DIM)
            xw = (act_ref[:, gcols] * w_exp[:, gcols]).astype(BF16)
            upd = lax.dot_general(b_g, xw, (((0,), (0,)), ((), ())), preferred_element_type=F32)
            s_ref[:, gcols] = s_ref[:, gcols] * cd_exp[:, gcols] + upd

    @pl.when(fwd_first)
    def _():
        conv_act(xf_ref, xfp_ref, xfn_ref, cf, cache_ref.at[slot_f])

    @pl.when(bwd_first)
    def _():
        conv_act(xb_ref, xbp_ref, xbn_ref, cbk, cache_ref.at[slot_b])

    direction(dtf_ref, 0, True, sf_ref, yf_ref, cache_ref.at[slot_f])
    direction(dtb_ref, SSD_HEADS, False, sb_ref, yb_ref, cache_ref.at[slot_b])


def _ssd(xbc, dt, conv_w, conv_b, dt_bias, a_log, d_skip_exp, expand, layer, n_batch, lat_len, ctx_len):
    rows = xbc.shape[0]
    n_lat = lat_len // CHUNK
    n_ctx = ctx_len // CHUNK
    ctx0 = n_batch * n_lat
    last16 = rows // HALO - 1
    per16 = CHUNK // HALO

    def fwd_blk(b, j):
        return jnp.where(j < n_ctx, ctx0 + b * n_ctx + j, b * n_lat + (j - n_ctx))

    def bwd_blk(b, j):
        return jnp.where(j < n_ctx, ctx0 + b * n_ctx + (n_ctx - 1 - j),
                         b * n_lat + (n_lat - 1 - (j - n_ctx)))

    def main(blk):
        return lambda b, j: (blk(b, j), 0)

    def prev(blk):
        return lambda b, j: (jnp.maximum(blk(b, j) * per16 - 1, 0), 0)

    def nxt(blk):
        return lambda b, j: (jnp.minimum(blk(b, j) * per16 + per16, last16), 0)

    const3 = lambda b, j: (layer, 0, 0)
    x_specs = []
    for blk in (fwd_blk, bwd_blk):
        x_specs += [pl.BlockSpec((CHUNK, XBC_DIM), main(blk)),
                    pl.BlockSpec((HALO, XBC_DIM), prev(blk)),
                    pl.BlockSpec((HALO, XBC_DIM), nxt(blk))]
    kern = functools.partial(_ssd_kernel, n_ctx=n_ctx, n_lat=n_lat)
    return pl.pallas_call(
        kern,
        out_shape=(jax.ShapeDtypeStruct((rows, SSD_DIM), F32),
                   jax.ShapeDtypeStruct((rows, SSD_DIM), F32)),
        grid=(n_batch, n_ctx + n_lat),
        in_specs=x_specs + [
            pl.BlockSpec((CHUNK, DT_PAD), main(fwd_blk)),
            pl.BlockSpec((CHUNK, DT_PAD), main(bwd_blk)),
            pl.BlockSpec((None, SSD_CONV_W, XBC_DIM), const3),
            pl.BlockSpec((None, 1, XBC_DIM), const3),
            pl.BlockSpec((None, 1, DT_PAD), const3),
            pl.BlockSpec((None, 1, DT_PAD), const3),
            pl.BlockSpec((None, 1, SSD_DIM), const3),
            pl.BlockSpec((2, 2 * LANES, SSD_DIM), lambda b, j: (0, 0, 0)),
        ],
        out_specs=(pl.BlockSpec((CHUNK, SSD_DIM), main(fwd_blk)),
                   pl.BlockSpec((CHUNK, SSD_DIM), main(bwd_blk))),
        scratch_shapes=[
            pltpu.VMEM((CHUNK + 16, XBC_DIM), F32),
            pltpu.VMEM((n_ctx + n_lat, CHUNK, XBC_DIM), F32),
            pltpu.VMEM((SSD_STATE, SSD_DIM), F32),
            pltpu.VMEM((SSD_STATE, SSD_DIM), F32),
        ],
        compiler_params=pltpu.CompilerParams(
            dimension_semantics=("arbitrary", "arbitrary"), vmem_limit_bytes=VMEM_LIMIT),
        name="ssd_scan",
    )(xbc, xbc, xbc, xbc, xbc, xbc, dt, dt, conv_w, conv_b, dt_bias, a_log, d_skip_exp, expand)


def _cm_kernel(cm_ref, w_ref, b_ref, g_ref, beta_ref, o_ref, sh_ref, conv_ref, *,
               n_lat_tiles, ctx_len):
    t = pl.program_id(0)
    tr = cm_ref.shape[0]
    half = CM_KERNEL // 2
    pad = CM_PAD
    assert pad - half + CM_KERNEL == CM_TAP_ROWS and CM_TAP_ROWS % SUBLANES == 0

    def run(seg_len):
        nseg = tr // seg_len
        stride = seg_len + 2 * pad
        assert stride % SUBLANES == 0 and seg_len % SUBLANES == 0
        zeros = jnp.zeros((pad, CM_DIM), F32)
        for s in range(nseg):
            rows = slice(s * seg_len, (s + 1) * seg_len)
            a = cm_ref[rows, 0:CM_DIM].astype(F32)
            gate = cm_ref[rows, CM_DIM:2 * CM_DIM].astype(F32)
            base = s * stride
            sh_ref[0, base:base + pad, :] = zeros
            sh_ref[0, base + pad:base + pad + seg_len, :] = a * jax.nn.sigmoid(gate)
            sh_ref[0, base + pad + seg_len:base + stride, :] = zeros
        span = nseg * stride - SUBLANES
        for lb in range(CM_DIM // LANES):
            cols = slice(lb * LANES, (lb + 1) * LANES)
            for r in range(1, SUBLANES):
                sh_ref[r, 0:span, cols] = sh_ref[0, r:r + span, cols]
        for lt0 in range(0, CM_DIM // LANES, CM_BLK_TILES):
            lane = [slice((lt0 + i) * LANES, (lt0 + i + 1) * LANES) for i in range(CM_BLK_TILES)]
            bias = [jnp.broadcast_to(b_ref[:, c], (SUBLANES, LANES)) for c in lane]
            for v0 in range(0, tr // SUBLANES, CM_BLK_ROWS):
                origin = []
                for v in range(CM_BLK_ROWS):
                    s, within = divmod((v0 + v) * SUBLANES, seg_len)
                    origin.append(s * stride + within)

                def taps(q, acc):
                    q8 = pl.multiple_of(q * SUBLANES, SUBLANES)
                    acc = list(acc)
                    for r in range(SUBLANES):
                        wk = [jnp.broadcast_to(w_ref[q8 + r, :, c], (SUBLANES, LANES)) for c in lane]
                        for v in range(CM_BLK_ROWS):
                            for i, c in enumerate(lane):
                                win = sh_ref[r, pl.ds(origin[v] + q8, SUBLANES), c]
                                acc[v * CM_BLK_TILES + i] = acc[v * CM_BLK_TILES + i] + wk[i] * win
                    return tuple(acc)

                acc = lax.fori_loop(0, CM_TAP_ROWS // SUBLANES, taps, tuple(bias) * CM_BLK_ROWS)
                for v in range(CM_BLK_ROWS):
                    rows = slice((v0 + v) * SUBLANES, (v0 + v + 1) * SUBLANES)
                    for i, c in enumerate(lane):
                        conv_ref[rows, c] = acc[v * CM_BLK_TILES + i]
        v = conv_ref[...]
        mu = jnp.mean(v, axis=-1, keepdims=True)
        xc = v - mu
        var = jnp.mean(xc * xc, axis=-1, keepdims=True)
        y = xc * lax.rsqrt(var + EPS) * g_ref[...] + beta_ref[...]
        o_ref[...] = _silu(y).astype(o_ref.dtype)

    @pl.when(t < n_lat_tiles)
    def _():
        run(GRID_W)

    @pl.when(t >= n_lat_tiles)
    def _():
        run(ctx_len)


def _conv_module(cm, dw_w, dw_b, ln_g, ln_b, layer, n_lat_rows, ctx_len, final):
    rows = n_lat_rows if final else cm.shape[0]
    tr = TR_CM
    assert ctx_len == tr and tr % GRID_W == 0
    const3 = lambda t: (layer, 0, 0)
    kern = functools.partial(_cm_kernel, n_lat_tiles=n_lat_rows // tr, ctx_len=ctx_len)
    pad_rows = max((tr // GRID_W) * (GRID_W + 32), ctx_len + 32)
    return pl.pallas_call(
        kern,
        out_shape=jax.ShapeDtypeStruct((rows, CM_DIM), BF16),
        grid=(rows // tr,),
        in_specs=[
            pl.BlockSpec((tr, 2 * CM_DIM), lambda t: (t, 0)),
            pl.BlockSpec((None, CM_TAP_ROWS, 1, CM_DIM), lambda t: (layer, 0, 0, 0)),
            pl.BlockSpec((None, 1, CM_DIM), const3),
            pl.BlockSpec((None, 1, CM_DIM), const3),
            pl.BlockSpec((None, 1, CM_DIM), const3),
        ],
        out_specs=pl.BlockSpec((tr, CM_DIM), lambda t: (t, 0)),
        scratch_shapes=[pltpu.VMEM((SUBLANES, pad_rows, CM_DIM), F32),
                        pltpu.VMEM((tr, CM_DIM), F32)],
        compiler_params=pltpu.CompilerParams(vmem_limit_bytes=VMEM_LIMIT),
        name="conv_module",
    )(cm, dw_w, dw_b, ln_g, ln_b)


def _outffn_kernel(x_ref, yf_ref, yb_ref, z_ref, cmo_ref, mod_ref, ng_ref, n2g_ref, fg_ref,
                   wo_ref, w1_ref, w2_ref, o_ref, act_ref, *, final):
    mod = lambda i: mod_ref[:, i * D_MODEL:(i + 1) * D_MODEL]
    v = (yf_ref[...] + yb_ref[...]) * _silu(z_ref[...].astype(F32))
    ms = jnp.mean(v * v, axis=-1, keepdims=True)
    so = (v * lax.rsqrt(ms + EPS) * ng_ref[...]).astype(BF16)
    mix = _dot(so, wo_ref[0:SSD_DIM, :]) + _dot(cmo_ref[...], wo_ref[SSD_DIM:SSD_DIM + CM_DIM, :])
    x1 = x_ref[...] + mod(2) * mix
    ms2 = jnp.mean(x1 * x1, axis=-1, keepdims=True)
    h2 = (x1 * lax.rsqrt(ms2 + EPS) * n2g_ref[...] * (1.0 + mod(4)) + mod(3)).astype(BF16)
    step = 256
    for c0 in range(0, FFN_HIDDEN, step):
        a = _dot(h2, w1_ref[:, c0:c0 + step])
        b = _dot(h2, w1_ref[:, FFN_HIDDEN + c0:FFN_HIDDEN + c0 + step])
        act_ref[:, c0:c0 + step] = (_silu(a) * b).astype(BF16)
    x2 = x1 + mod(5) * _dot(act_ref[...], w2_ref[...])
    if final:
        msf = jnp.mean(x2 * x2, axis=-1, keepdims=True)
        x2 = x2 * lax.rsqrt(msf + EPS) * fg_ref[...]
    o_ref[...] = x2


def _outffn(xall, yf, yb, z, cmo, mods, ssd_norm_g, norm2_g, final_g, w_out_b, w1_b, w2_b,
            layer, n_lat_rows, lat_len, final):
    tm = TM_OUT
    rows = n_lat_rows if final else xall.shape[0]
    n_lat_tiles = n_lat_rows // tm
    tiles_per_batch = lat_len // tm
    n_batch = n_lat_rows // lat_len

    def mod_idx(t):
        return jnp.where(t < n_lat_tiles, t // tiles_per_batch, n_batch)

    row_spec = lambda width: pl.BlockSpec((tm, width), lambda t: (t, 0))
    const3 = lambda t: (layer, 0, 0)
    single = pl.Buffered(1)
    kern = functools.partial(_outffn_kernel, final=final)
    return pl.pallas_call(
        kern,
        out_shape=jax.ShapeDtypeStruct((rows, D_MODEL), F32),
        grid=(rows // tm,),
        in_specs=[
            row_spec(D_MODEL), row_spec(SSD_DIM), row_spec(SSD_DIM), row_spec(SSD_DIM),
            row_spec(CM_DIM),
            pl.BlockSpec((None, None, 1, 6 * D_MODEL), lambda t: (layer, mod_idx(t), 0, 0)),
            pl.BlockSpec((None, 1, SSD_DIM), const3),
            pl.BlockSpec((None, 1, D_MODEL), const3),
            pl.BlockSpec((1, D_MODEL), lambda t: (0, 0)),
            pl.BlockSpec((None, SSD_DIM + CM_DIM, D_MODEL), const3, pipeline_mode=single),
            pl.BlockSpec((None, D_MODEL, 2 * FFN_HIDDEN), const3, pipeline_mode=single),
            pl.BlockSpec((None, FFN_HIDDEN, D_MODEL), const3, pipeline_mode=single),
        ],
        out_specs=row_spec(D_MODEL),
        scratch_shapes=[pltpu.VMEM((tm, FFN_HIDDEN), BF16)],
        compiler_params=pltpu.CompilerParams(vmem_limit_bytes=VMEM_LIMIT),
        name="outproj_ffn",
    )(xall, yf, yb, z, cmo, mods, ssd_norm_g, norm2_g, final_g, w_out_b, w1_b, w2_b)


def kernel(x, c, ctx, c_ctx, w_in, ssd_conv_w, ssd_conv_b, dt_bias, a_log, d_skip, ssd_norm_g,
           cm_dw_w, cm_dw_b, cm_ln_g, cm_ln_b, w_out, w_ffn_in, w_ffn_out, ada_w, ada_b,
           norm1_g, norm2_g, final_norm_g):
    n_batch, lat_len, _ = x.shape
    ctx_len = ctx.shape[1]
    depth = w_in.shape[0]
    n_lat_rows = n_batch * lat_len
    assert lat_len % TM_IN == 0 and (n_batch * ctx_len) % TM_IN == 0
    assert lat_len % CHUNK == 0 and ctx_len % CHUNK == 0 and n_batch + 1 <= 16

    xall = jnp.concatenate([x.reshape(n_lat_rows, D_MODEL),
                            ctx.reshape(n_batch * ctx_len, D_MODEL)], axis=0)

    c_all = jnp.concatenate([c, c_ctx[None, :],
                             jnp.zeros((16 - n_batch - 1, D_MODEL), F32)], axis=0)
    mods = _mods(c_all, ada_w, ada_b).reshape(depth, 16, 1, 6 * D_MODEL)

    i0, i1, i2 = SSD_DIM, SSD_DIM + XBC_DIM, SSD_DIM + XBC_DIM + 2 * SSD_HEADS
    w_in_b = jnp.concatenate(
        [w_in[..., :i1], w_in[..., i2:], w_in[..., i1:i2],
         jnp.zeros((depth, D_MODEL, DT_PAD - 2 * SSD_HEADS), F32)], axis=-1).astype(BF16)
    w_out_b = w_out.astype(BF16)
    w1_b = w_ffn_in.astype(BF16)
    w2_b = w_ffn_out.astype(BF16)
    pad_lanes = lambda v: jnp.pad(v.reshape(depth, 1, 2 * SSD_HEADS),
                                  ((0, 0), (0, 0), (0, DT_PAD - 2 * SSD_HEADS)))
    dt_bias_p = pad_lanes(dt_bias)
    a_log_p = pad_lanes(a_log)
    d_skip_exp = jnp.repeat(d_skip, SSD_HEAD_DIM, axis=-1).reshape(depth, 1, SSD_DIM)
    row3 = lambda v: v.reshape(depth, 1, v.shape[-1])
    cm_w_rows = jnp.pad(cm_dw_w, ((0, 0), (CM_TAP_ROWS - CM_KERNEL, 0), (0, 0))).reshape(
        depth, CM_TAP_ROWS, 1, CM_DIM)

    r = jnp.arange(2 * LANES)[:, None] % LANES
    col_head = jnp.arange(SSD_DIM)[None, :] // SSD_HEAD_DIM
    expand = jnp.stack([(r == col_head), (r == col_head + SSD_HEADS)]).astype(BF16)

    for i in range(depth):
        final = i == depth - 1
        z, xbc, cm, dt = _inproj(xall, mods, row3(norm1_g), w_in_b, i, n_lat_rows, lat_len)
        yf, yb = _ssd(xbc, dt, ssd_conv_w, row3(ssd_conv_b), dt_bias_p, a_log_p, d_skip_exp,
                      expand, i, n_batch, lat_len, ctx_len)
        cmo = _conv_module(cm, cm_w_rows, row3(cm_dw_b),
                           row3(cm_ln_g), row3(cm_ln_b), i,
                           n_lat_rows, ctx_len, final)
        xall = _outffn(xall, yf, yb, z, cmo, mods, row3(ssd_norm_g), row3(norm2_g),
                       final_norm_g.reshape(1, D_MODEL), w_out_b, w1_b, w2_b,
                       i, n_lat_rows, lat_len, final)
    return xall.reshape(n_batch, lat_len, D_MODEL)
```

```python
import functools

import jax
import jax.numpy as jnp
from jax import lax
from jax.experimental import pallas as pl
from jax.experimental.pallas import tpu as pltpu

F32 = jnp.float32
BF16 = jnp.bfloat16

D_MODEL = 1024
SSD_HEADS = 16
SSD_HEAD_DIM = 64
SSD_DIM = SSD_HEADS * SSD_HEAD_DIM
SSD_GROUPS = 2
SSD_STATE = 128
SSD_CONV_W = 5
CHUNK = 128
XBC_DIM = SSD_DIM + 2 * SSD_GROUPS * SSD_STATE
CM_DIM = D_MODEL
CM_KERNEL = 31
GRID_W = 64
FFN_HIDDEN = 2816
EPS = 1e-6
LOG2_E = 1.4426950408889634

LANES = 128
SUBLANES = 8
DT_PAD = LANES
IN_COLS = SSD_DIM + XBC_DIM + 2 * CM_DIM + DT_PAD
HALO = 16
VMEM_LIMIT = 56 * 1024 * 1024

TM_IN = 512
TM_OUT = 512
TR_CM = 256
CM_BLK_ROWS = 4
CM_BLK_TILES = 4
CM_PAD = 16
CM_TAP_ROWS = 32


def _silu(v):
    return v * jax.nn.sigmoid(v)


def _dot(a, b):
    return jnp.dot(a, b, preferred_element_type=F32)


def _mods_kernel(c_ref, w_ref, b_ref, o_ref):
    sc = _silu(c_ref[...])
    o_ref[...] = _dot(sc.astype(BF16), w_ref[...].astype(BF16)) + b_ref[...]


def _mods(c_all, ada_w, ada_b):
    depth = ada_w.shape[0]
    nrow = c_all.shape[0]
    tn = 1536
    return pl.pallas_call(
        _mods_kernel,
        out_shape=jax.ShapeDtypeStruct((depth, nrow, 6 * D_MODEL), F32),
        grid=(depth, 6 * D_MODEL // tn),
        in_specs=[
            pl.BlockSpec((nrow, D_MODEL), lambda i, j: (0, 0)),
            pl.BlockSpec((None, D_MODEL, tn), lambda i, j: (i, 0, j)),
            pl.BlockSpec((None, 1, tn), lambda i, j: (i, 0, j)),
        ],
        out_specs=pl.BlockSpec((None, nrow, tn), lambda i, j: (i, 0, j)),
        compiler_params=pltpu.CompilerParams(vmem_limit_bytes=VMEM_LIMIT),
        name="adaln_mods",
    )(c_all, ada_w, ada_b.reshape(depth, 1, 6 * D_MODEL))


CM_OFF = SSD_DIM + XBC_DIM
DT_OFF = CM_OFF + 2 * CM_DIM
MM_N = 256


def _inproj_cm_kernel(x_ref, mod_ref, g_ref, w_ref, tap_ref, cb_ref, lng_ref, lnb_ref,
                      z_ref, xbc_ref, dt_ref, cmo_ref, h_ref, sh_ref, conv_ref, *,
                      n_lat_tiles, ctx_len):
    t = pl.program_id(0)
    tm = x_ref.shape[0]
    half_rows = TR_CM
    half = CM_KERNEL // 2
    pad = CM_PAD
    assert pad - half + CM_KERNEL == CM_TAP_ROWS and CM_TAP_ROWS % SUBLANES == 0
    blk_rows = CM_BLK_ROWS * SUBLANES
    blk_lanes = CM_BLK_TILES * LANES
    lane_groups = CM_DIM // blk_lanes
    n_blocks = (half_rows // blk_rows) * lane_groups
    n_z, n_xbc = SSD_DIM // MM_N, XBC_DIM // MM_N

    x = x_ref[...]
    ms = jnp.mean(x * x, axis=-1, keepdims=True)
    y = x * lax.rsqrt(ms + EPS) * g_ref[...]
    sh1 = mod_ref[:, 0:D_MODEL]
    s1 = mod_ref[:, D_MODEL:2 * D_MODEL]
    h_ref[...] = (y * (1.0 + s1) + sh1).astype(BF16)
    dt_ref[...] = _dot(h_ref[...], w_ref[:, DT_OFF:DT_OFF + DT_PAD])

    def run(seg_len):
        nseg = half_rows // seg_len
        stride = seg_len + 2 * pad
        assert stride % SUBLANES == 0 and seg_len % blk_rows == 0
        zeros = jnp.zeros((pad, CM_DIM), F32)
        span = nseg * stride - SUBLANES

        def conv_block(i):
            aligned = (lambda v, m: v) if isinstance(i, int) else pl.multiple_of
            lane0 = aligned((i % lane_groups) * blk_lanes, blk_lanes)
            row0 = (i // lane_groups) * blk_rows
            seg = row0 // seg_len
            origin0 = aligned(seg * stride + (row0 - seg * seg_len), SUBLANES)
            lanes = [pl.ds(lane0 + k * LANES, LANES) for k in range(CM_BLK_TILES)]
            acc = [jnp.broadcast_to(cb_ref[:, c], (SUBLANES, LANES)) for c in lanes] * CM_BLK_ROWS
            for off in range(CM_TAP_ROWS):
                r, q8 = off % SUBLANES, off - off % SUBLANES
                wk = [jnp.broadcast_to(tap_ref[off, :, c], (SUBLANES, LANES)) for c in lanes]
                for v in range(CM_BLK_ROWS):
                    for k, c in enumerate(lanes):
                        win = sh_ref[r, pl.ds(origin0 + v * SUBLANES + q8, SUBLANES), c]
                        acc[v * CM_BLK_TILES + k] = acc[v * CM_BLK_TILES + k] + wk[k] * win
            out0 = aligned(row0, SUBLANES)
            for v in range(CM_BLK_ROWS):
                for k, c in enumerate(lanes):
                    conv_ref[pl.ds(out0 + v * SUBLANES, SUBLANES), c] = acc[v * CM_BLK_TILES + k]

        for hf in range(tm // half_rows):
            rows = slice(hf * half_rows, (hf + 1) * half_rows)

            def piece(c0, width=MM_N):
                return _dot(h_ref[rows, :], w_ref[:, pl.ds(c0, width)])

            for c0 in range(0, CM_DIM, MM_N):
                u = piece(CM_OFF + c0) * jax.nn.sigmoid(piece(CM_OFF + CM_DIM + c0))
                for s in range(nseg):
                    sh_ref[0, s * stride + pad:s * stride + pad + seg_len, c0:c0 + MM_N] = (
                        u[s * seg_len:(s + 1) * seg_len])
            for s in range(nseg):
                sh_ref[0, s * stride:s * stride + pad, :] = zeros
                sh_ref[0, s * stride + pad + seg_len:(s + 1) * stride, :] = zeros
            for lt0 in range(0, CM_DIM // LANES, CM_BLK_TILES):
                cols = slice(lt0 * LANES, (lt0 + CM_BLK_TILES) * LANES)
                for r in range(1, SUBLANES):
                    sh_ref[r, 0:span, cols] = sh_ref[0, r:r + span, cols]

            pieces = ([(z_ref, j * MM_N, j * MM_N) for j in range(n_z)]
                      + [(xbc_ref, j * MM_N, SSD_DIM + j * MM_N) for j in range(n_xbc)])
            n_static = min(len(pieces), n_blocks)
            for i in range(n_static):
                for ref, dst0, src0 in pieces[i::n_static]:
                    ref[rows, dst0:dst0 + MM_N] = piece(src0).astype(ref.dtype)
                conv_block(i)

            def conv_only(i, carry):
                conv_block(i)
                return carry

            lax.fori_loop(n_static, n_blocks, conv_only, 0)

            v = conv_ref[...]
            mu = jnp.mean(v, axis=-1, keepdims=True)
            xc = v - mu
            var = jnp.mean(xc * xc, axis=-1, keepdims=True)
            yln = xc * lax.rsqrt(var + EPS) * lng_ref[...] + lnb_ref[...]
            cmo_ref[rows, :] = _silu(yln).astype(cmo_ref.dtype)

    @pl.when(t < n_lat_tiles)
    def _():
        run(GRID_W)

    @pl.when(t >= n_lat_tiles)
    def _():
        run(ctx_len)


def _inproj_cm(xall, mods, norm_g, w_in_b, cm_w_rows, cm_b, ln_g, ln_b, layer, n_lat_rows,
               lat_len, ctx_len):
    rows = xall.shape[0]
    tm = TM_IN
    assert tm % TR_CM == 0 and TR_CM % ctx_len == 0 and TR_CM % GRID_W == 0
    n_lat_tiles = n_lat_rows // tm
    tiles_per_batch = lat_len // tm
    n_batch = n_lat_rows // lat_len

    def mod_idx(t):
        return jnp.where(t < n_lat_tiles, t // tiles_per_batch, n_batch)

    const3 = lambda t: (layer, 0, 0)
    kern = functools.partial(_inproj_cm_kernel, n_lat_tiles=n_lat_tiles, ctx_len=ctx_len)
    pad_rows = max((TR_CM // GRID_W) * (GRID_W + 2 * CM_PAD), ctx_len + 2 * CM_PAD)
    return pl.pallas_call(
        kern,
        out_shape=(
            jax.ShapeDtypeStruct((rows, SSD_DIM), BF16),
            jax.ShapeDtypeStruct((rows, XBC_DIM), BF16),
            jax.ShapeDtypeStruct((rows, DT_PAD), F32),
            jax.ShapeDtypeStruct((rows, CM_DIM), BF16),
        ),
        grid=(rows // tm,),
        in_specs=[
            pl.BlockSpec((tm, D_MODEL), lambda t: (t, 0)),
            pl.BlockSpec((None, None, 1, 6 * D_MODEL), lambda t: (layer, mod_idx(t), 0, 0)),
            pl.BlockSpec((None, 1, D_MODEL), const3),
            pl.BlockSpec((None, D_MODEL, IN_COLS), const3, pipeline_mode=pl.Buffered(1)),
            pl.BlockSpec((None, CM_TAP_ROWS, 1, CM_DIM), lambda t: (layer, 0, 0, 0)),
            pl.BlockSpec((None, 1, CM_DIM), const3),
            pl.BlockSpec((None, 1, CM_DIM), const3),
            pl.BlockSpec((None, 1, CM_DIM), const3),
        ],
        out_specs=(
            pl.BlockSpec((tm, SSD_DIM), lambda t: (t, 0)),
            pl.BlockSpec((tm, XBC_DIM), lambda t: (t, 0)),
            pl.BlockSpec((tm, DT_PAD), lambda t: (t, 0)),
            pl.BlockSpec((tm, CM_DIM), lambda t: (t, 0)),
        ),
        scratch_shapes=[pltpu.VMEM((tm, D_MODEL), BF16),
                        pltpu.VMEM((SUBLANES, pad_rows, CM_DIM), F32),
                        pltpu.VMEM((TR_CM, CM_DIM), F32)],
        compiler_params=pltpu.CompilerParams(vmem_limit_bytes=VMEM_LIMIT),
        name="inproj_convmod",
    )(xall, mods, norm_g, w_in_b, cm_w_rows, cm_b, ln_g, ln_b)


def _split_hi_mid_lo(v):
    hi = v.astype(BF16)
    r = v - hi.astype(F32)
    mid = r.astype(BF16)
    lo = (r - mid.astype(F32)).astype(BF16)
    return hi, mid, lo


def _ssd_kernel(xf_ref, xfp_ref, xfn_ref, xb_ref, xbp_ref, xbn_ref, dtf_ref, dtb_ref,
                cw_ref, cb_ref, dtbias_ref, alog_ref, dskip_ref, exp_ref,
                yf_ref, yb_ref, ext_ref, cache_ref, sf_ref, sb_ref, *, n_ctx, n_lat):
    j = pl.program_id(1)

    @pl.when(j == 0)
    def _():
        sf_ref[...] = jnp.zeros_like(sf_ref)
        sb_ref[...] = jnp.zeros_like(sb_ref)

    is_ctx = j < n_ctx
    nchunks = jnp.where(is_ctx, n_ctx, n_lat)
    cf = jnp.where(is_ctx, j, j - n_ctx)
    cbk = nchunks - 1 - cf
    seq_base = jnp.where(is_ctx, 0, n_ctx)
    slot_f = seq_base + cf
    slot_b = seq_base + cbk
    fwd_first = 2 * cf <= nchunks - 1
    bwd_first = 2 * cf < nchunks - 1

    ri = lax.broadcasted_iota(jnp.int32, (CHUNK, CHUNK), 0)
    ci = lax.broadcasted_iota(jnp.int32, (CHUNK, CHUNK), 1)
    low_mask = ri >= ci
    up_mask = ri <= ci
    lane_lo = ci < SSD_HEAD_DIM
    a_neg = -jnp.exp(alog_ref[...])

    def conv_act(x_ref, xp_ref, xn_ref, cidx, act_ref):
        prev = xp_ref[...].astype(F32)[HALO - 8:HALO]
        nxt = xn_ref[...].astype(F32)[0:8]
        ext_ref[0:8, :] = jnp.where(cidx > 0, prev, 0.0)
        ext_ref[8:8 + CHUNK, :] = x_ref[...].astype(F32)
        ext_ref[8 + CHUNK:16 + CHUNK, :] = jnp.where(cidx < nchunks - 1, nxt, 0.0)
        for lb in range(XBC_DIM // LANES):
            cols = slice(lb * LANES, (lb + 1) * LANES)
            acc = jnp.broadcast_to(cb_ref[:, cols], (CHUNK, LANES))
            for k in range(SSD_CONV_W):
                acc = acc + cw_ref[k:k + 1, cols] * ext_ref[6 + k:6 + k + CHUNK, cols]
            act_ref[:, cols] = _silu(acc)

    def direction(dt_ref, col_off, forward, s_ref, y_ref, act_ref):
        dtv = jax.nn.softplus(dt_ref[...] + dtbias_ref[...])
        dta = dtv * (a_neg * LOG2_E)
        tri = jnp.where(low_mask if forward else up_mask, 1.0, 0.0).astype(BF16)
        hi, mid, lo = _split_hi_mid_lo(dta)
        acs = _dot(tri, hi) + _dot(tri, mid) + _dot(tri, lo)
        used = slice(0, 2 * SSD_HEADS)
        acs_row = acs.T[used] - jnp.log2(dtv.T[used])
        edge = acs[CHUNK - 1:CHUNK, :] if forward else acs[0:1, :]
        wst = dtv * jnp.exp2(edge - acs)
        cd16 = jnp.broadcast_to(jnp.exp2(edge), (16, LANES))
        v = jnp.concatenate([wst, cd16], axis=0)
        v_hi = v.astype(BF16)
        v_mid = (v - v_hi.astype(F32)).astype(BF16)
        expanded = _dot(jnp.concatenate([v_hi, v_mid], axis=1),
                        exp_ref[0 if forward else 1])
        w_exp = expanded[0:CHUNK]
        cd_exp = expanded[CHUNK:CHUNK + 1]

        mask = low_mask if forward else up_mask
        for g in range(SSD_GROUPS):
            b_g = act_ref[:, SSD_DIM + g * SSD_STATE:SSD_DIM + (g + 1) * SSD_STATE].astype(BF16)
            c_off = SSD_DIM + SSD_GROUPS * SSD_STATE + g * SSD_STATE
            c_f32 = act_ref[:, c_off:c_off + SSD_STATE]
            gmat = lax.dot_general(c_f32.astype(BF16), b_g, (((1,), (1,)), ((), ())),
                                   preferred_element_type=F32)
            heads_per_group = SSD_HEADS // SSD_GROUPS
            for pair in range(heads_per_group // 2):
                h0 = g * heads_per_group + 2 * pair
                cols = slice(h0 * SSD_HEAD_DIM, (h0 + 2) * SSD_HEAD_DIM)
                xs_pair = act_ref[:, cols]
                rhs = jnp.concatenate([xs_pair.astype(BF16), s_ref[:, cols].astype(BF16)], axis=0)
                res = []
                for h in (h0, h0 + 1):
                    c = col_off + h
                    col = jnp.broadcast_to(acs[:, c:c + 1], (CHUNK, CHUNK))
                    seg = col - acs_row[c:c + 1, :]
                    dec = jnp.exp2(jnp.where(mask, seg, -jnp.inf))
                    sc = (gmat * dec).astype(BF16)
                    ce = (c_f32 * jnp.exp2(col)).astype(BF16)
                    res.append(_dot(jnp.concatenate([sc, ce], axis=1), rhs))
                y = jnp.where(lane_lo, res[0], res[1])
                if forward:
                    y = y + dskip_ref[:, cols] * xs_pair
                y_ref[:, cols] = y
            gcols = slice(g * heads_per_group * SSD_HEAD_DIM, (g + 1) * heads_per_group * SSD_HEAD_DIM)
            xw = (act_ref[:, gcols] * w_exp[:, gcols]).astype(BF16)
            upd = lax.dot_general(b_g, xw, (((0,), (0,)), ((), ())), preferred_element_type=F32)
            s_ref[:, gcols] = s_ref[:, gcols] * cd_exp[:, gcols] + upd

    @pl.when(fwd_first)
    def _():
        conv_act(xf_ref, xfp_ref, xfn_ref, cf, cache_ref.at[slot_f])

    @pl.when(bwd_first)
    def _():
        conv_act(xb_ref, xbp_ref, xbn_ref, cbk, cache_ref.at[slot_b])

    direction(dtf_ref, 0, True, sf_ref, yf_ref, cache_ref.at[slot_f])
    direction(dtb_ref, SSD_HEADS, False, sb_ref, yb_ref, cache_ref.at[slot_b])


def _ssd(xbc, dt, conv_w, conv_b, dt_bias, a_log, d_skip_exp, expand, layer, n_batch, lat_len, ctx_len):
    rows = xbc.shape[0]
    n_lat = lat_len // CHUNK
    n_ctx = ctx_len // CHUNK
    ctx0 = n_batch * n_lat
    last16 = rows // HALO - 1
    per16 = CHUNK // HALO

    def fwd_blk(b, j):
        return jnp.where(j < n_ctx, ctx0 + b * n_ctx + j, b * n_lat + (j - n_ctx))

    def bwd_blk(b, j):
        return jnp.where(j < n_ctx, ctx0 + b * n_ctx + (n_ctx - 1 - j),
                         b * n_lat + (n_lat - 1 - (j - n_ctx)))

    def main(blk):
        return lambda b, j: (blk(b, j), 0)

    def prev(blk):
        return lambda b, j: (jnp.maximum(blk(b, j) * per16 - 1, 0), 0)

    def nxt(blk):
        return lambda b, j: (jnp.minimum(blk(b, j) * per16 + per16, last16), 0)

    const3 = lambda b, j: (layer, 0, 0)
    x_specs = []
    for blk in (fwd_blk, bwd_blk):
        x_specs += [pl.BlockSpec((CHUNK, XBC_DIM), main(blk)),
                    pl.BlockSpec((HALO, XBC_DIM), prev(blk)),
                    pl.BlockSpec((HALO, XBC_DIM), nxt(blk))]
    kern = functools.partial(_ssd_kernel, n_ctx=n_ctx, n_lat=n_lat)
    return pl.pallas_call(
        kern,
        out_shape=(jax.ShapeDtypeStruct((rows, SSD_DIM), F32),
                   jax.ShapeDtypeStruct((rows, SSD_DIM), F32)),
        grid=(n_batch, n_ctx + n_lat),
        in_specs=x_specs + [
            pl.BlockSpec((CHUNK, DT_PAD), main(fwd_blk)),
            pl.BlockSpec((CHUNK, DT_PAD), main(bwd_blk)),
            pl.BlockSpec((None, SSD_CONV_W, XBC_DIM), const3),
            pl.BlockSpec((None, 1, XBC_DIM), const3),
            pl.BlockSpec((None, 1, DT_PAD), const3),
            pl.BlockSpec((None, 1, DT_PAD), const3),
            pl.BlockSpec((None, 1, SSD_DIM), const3),
            pl.BlockSpec((2, 2 * LANES, SSD_DIM), lambda b, j: (0, 0, 0)),
        ],
        out_specs=(pl.BlockSpec((CHUNK, SSD_DIM), main(fwd_blk)),
                   pl.BlockSpec((CHUNK, SSD_DIM), main(bwd_blk))),
        scratch_shapes=[
            pltpu.VMEM((CHUNK + 16, XBC_DIM), F32),
            pltpu.VMEM((n_ctx + n_lat, CHUNK, XBC_DIM), F32),
            pltpu.VMEM((SSD_STATE, SSD_DIM), F32),
            pltpu.VMEM((SSD_STATE, SSD_DIM), F32),
        ],
        compiler_params=pltpu.CompilerParams(
            dimension_semantics=("arbitrary", "arbitrary"), vmem_limit_bytes=VMEM_LIMIT),
        name="ssd_scan",
    )(xbc, xbc, xbc, xbc, xbc, xbc, dt, dt, conv_w, conv_b, dt_bias, a_log, d_skip_exp, expand)


def _outffn_kernel(x_ref, yf_ref, yb_ref, z_ref, cmo_ref, mod_ref, ng_ref, n2g_ref, fg_ref,
                   wo_ref, w1_ref, w2_ref, o_ref, act_ref, *, final):
    mod = lambda i: mod_ref[:, i * D_MODEL:(i + 1) * D_MODEL]
    v = (yf_ref[...] + yb_ref[...]) * _silu(z_ref[...].astype(F32))
    ms = jnp.mean(v * v, axis=-1, keepdims=True)
    so = (v * lax.rsqrt(ms + EPS) * ng_ref[...]).astype(BF16)
    mix = _dot(so, wo_ref[0:SSD_DIM, :]) + _dot(cmo_ref[...], wo_ref[SSD_DIM:SSD_DIM + CM_DIM, :])
    x1 = x_ref[...] + mod(2) * mix
    ms2 = jnp.mean(x1 * x1, axis=-1, keepdims=True)
    h2 = (x1 * lax.rsqrt(ms2 + EPS) * n2g_ref[...] * (1.0 + mod(4)) + mod(3)).astype(BF16)
    step = 256
    for c0 in range(0, FFN_HIDDEN, step):
        a = _dot(h2, w1_ref[:, c0:c0 + step])
        b = _dot(h2, w1_ref[:, FFN_HIDDEN + c0:FFN_HIDDEN + c0 + step])
        act_ref[:, c0:c0 + step] = (_silu(a) * b).astype(BF16)
    x2 = x1 + mod(5) * _dot(act_ref[...], w2_ref[...])
    if final:
        msf = jnp.mean(x2 * x2, axis=-1, keepdims=True)
        x2 = x2 * lax.rsqrt(msf + EPS) * fg_ref[...]
    o_ref[...] = x2


def _outffn(xall, yf, yb, z, cmo, mods, ssd_norm_g, norm2_g, final_g, w_out_b, w1_b, w2_b,
            layer, n_lat_rows, lat_len, final):
    tm = TM_OUT
    rows = n_lat_rows if final else xall.shape[0]
    n_lat_tiles = n_lat_rows // tm
    tiles_per_batch = lat_len // tm
    n_batch = n_lat_rows // lat_len

    def mod_idx(t):
        return jnp.where(t < n_lat_tiles, t // tiles_per_batch, n_batch)

    row_spec = lambda width: pl.BlockSpec((tm, width), lambda t: (t, 0))
    const3 = lambda t: (layer, 0, 0)
    single = pl.Buffered(1)
    kern = functools.partial(_outffn_kernel, final=final)
    return pl.pallas_call(
        kern,
        out_shape=jax.ShapeDtypeStruct((rows, D_MODEL), F32),
        grid=(rows // tm,),
        in_specs=[
            row_spec(D_MODEL), row_spec(SSD_DIM), row_spec(SSD_DIM), row_spec(SSD_DIM),
            row_spec(CM_DIM),
            pl.BlockSpec((None, None, 1, 6 * D_MODEL), lambda t: (layer, mod_idx(t), 0, 0)),
            pl.BlockSpec((None, 1, SSD_DIM), const3),
            pl.BlockSpec((None, 1, D_MODEL), const3),
            pl.BlockSpec((1, D_MODEL), lambda t: (0, 0)),
            pl.BlockSpec((None, SSD_DIM + CM_DIM, D_MODEL), const3, pipeline_mode=single),
            pl.BlockSpec((None, D_MODEL, 2 * FFN_HIDDEN), const3, pipeline_mode=single),
            pl.BlockSpec((None, FFN_HIDDEN, D_MODEL), const3, pipeline_mode=single),
        ],
        out_specs=row_spec(D_MODEL),
        scratch_shapes=[pltpu.VMEM((tm, FFN_HIDDEN), BF16)],
        compiler_params=pltpu.CompilerParams(vmem_limit_bytes=VMEM_LIMIT),
        name="outproj_ffn",
    )(xall, yf, yb, z, cmo, mods, ssd_norm_g, norm2_g, final_g, w_out_b, w1_b, w2_b)


def kernel(x, c, ctx, c_ctx, w_in, ssd_conv_w, ssd_conv_b, dt_bias, a_log, d_skip, ssd_norm_g,
           cm_dw_w, cm_dw_b, cm_ln_g, cm_ln_b, w_out, w_ffn_in, w_ffn_out, ada_w, ada_b,
           norm1_g, norm2_g, final_norm_g):
    n_batch, lat_len, _ = x.shape
    ctx_len = ctx.shape[1]
    depth = w_in.shape[0]
    n_lat_rows = n_batch * lat_len
    assert lat_len % TM_IN == 0 and (n_batch * ctx_len) % TM_IN == 0
    assert lat_len % CHUNK == 0 and ctx_len % CHUNK == 0 and n_batch + 1 <= 16

    xall = jnp.concatenate([x.reshape(n_lat_rows, D_MODEL),
                            ctx.reshape(n_batch * ctx_len, D_MODEL)], axis=0)

    c_all = jnp.concatenate([c, c_ctx[None, :],
                             jnp.zeros((16 - n_batch - 1, D_MODEL), F32)], axis=0)
    mods = _mods(c_all, ada_w, ada_b).reshape(depth, 16, 1, 6 * D_MODEL)

    i0, i1, i2 = SSD_DIM, SSD_DIM + XBC_DIM, SSD_DIM + XBC_DIM + 2 * SSD_HEADS
    w_in_b = jnp.concatenate(
        [w_in[..., :i1], w_in[..., i2:], w_in[..., i1:i2],
         jnp.zeros((depth, D_MODEL, DT_PAD - 2 * SSD_HEADS), F32)], axis=-1).astype(BF16)
    w_out_b = w_out.astype(BF16)
    w1_b = w_ffn_in.astype(BF16)
    w2_b = w_ffn_out.astype(BF16)
    pad_lanes = lambda v: jnp.pad(v.reshape(depth, 1, 2 * SSD_HEADS),
                                  ((0, 0), (0, 0), (0, DT_PAD - 2 * SSD_HEADS)))
    dt_bias_p = pad_lanes(dt_bias)
    a_log_p = pad_lanes(a_log)
    d_skip_exp = jnp.repeat(d_skip, SSD_HEAD_DIM, axis=-1).reshape(depth, 1, SSD_DIM)
    row3 = lambda v: v.reshape(depth, 1, v.shape[-1])
    cm_w_rows = jnp.pad(cm_dw_w, ((0, 0), (CM_TAP_ROWS - CM_KERNEL, 0), (0, 0))).reshape(
        depth, CM_TAP_ROWS, 1, CM_DIM)

    r = jnp.arange(2 * LANES)[:, None] % LANES
    col_head = jnp.arange(SSD_DIM)[None, :] // SSD_HEAD_DIM
    expand = jnp.stack([(r == col_head), (r == col_head + SSD_HEADS)]).astype(BF16)

    for i in range(depth):
        final = i == depth - 1
        z, xbc, dt, cmo = _inproj_cm(xall, mods, row3(norm1_g), w_in_b, cm_w_rows, row3(cm_dw_b),
                                     row3(cm_ln_g), row3(cm_ln_b), i, n_lat_rows, lat_len, ctx_len)
        yf, yb = _ssd(xbc, dt, ssd_conv_w, row3(ssd_conv_b), dt_bias_p, a_log_p, d_skip_exp,
                      expand, i, n_batch, lat_len, ctx_len)
        xall = _outffn(xall, yf, yb, z, cmo, mods, row3(ssd_norm_g), row3(norm2_g),
                       final_norm_g.reshape(1, D_MODEL), w_out_b, w1_b, w2_b,
                       i, n_lat_rows, lat_len, final)
    return xall.reshape(n_batch, lat_len, D_MODEL)
```

```python
import functools

import jax
import jax.numpy as jnp
from jax import lax
from jax.experimental import pallas as pl
from jax.experimental.pallas import tpu as pltpu

F32 = jnp.float32
BF16 = jnp.bfloat16

D_MODEL = 1024
SSD_HEADS = 16
SSD_HEAD_DIM = 64
SSD_DIM = SSD_HEADS * SSD_HEAD_DIM
SSD_GROUPS = 2
SSD_STATE = 128
SSD_CONV_W = 5
CHUNK = 128
XBC_DIM = SSD_DIM + 2 * SSD_GROUPS * SSD_STATE
CM_DIM = D_MODEL
CM_KERNEL = 31
GRID_W = 64
FFN_HIDDEN = 2816
EPS = 1e-6
LOG2_E = 1.4426950408889634

LANES = 128
SUBLANES = 8
DT_PAD = LANES
IN_COLS = SSD_DIM + XBC_DIM + 2 * CM_DIM + DT_PAD
HALO = 16
VMEM_LIMIT = 56 * 1024 * 1024

TM_IN = 512
TM_MIX = 256
CM_BLK_ROWS = 4
CM_BLK_TILES = 4
CM_PAD = 16
CM_TAP_ROWS = 32


def _silu(v):
    return v * jax.nn.sigmoid(v)


def _dot(a, b):
    return jnp.dot(a, b, preferred_element_type=F32)


def _sched_zero(v):
    bits = pltpu.bitcast(v, jnp.uint32)
    return pltpu.bitcast(lax.shift_right_logical(bits, jnp.uint32(32)), F32)


def _mods_kernel(c_ref, w_ref, b_ref, o_ref):
    sc = _silu(c_ref[...])
    o_ref[...] = _dot(sc.astype(BF16), w_ref[...].astype(BF16)) + b_ref[...]


def _mods(c_all, ada_w, ada_b):
    depth = ada_w.shape[0]
    nrow = c_all.shape[0]
    tn = 1536
    return pl.pallas_call(
        _mods_kernel,
        out_shape=jax.ShapeDtypeStruct((depth, nrow, 6 * D_MODEL), F32),
        grid=(depth, 6 * D_MODEL // tn),
        in_specs=[
            pl.BlockSpec((nrow, D_MODEL), lambda i, j: (0, 0)),
            pl.BlockSpec((None, D_MODEL, tn), lambda i, j: (i, 0, j)),
            pl.BlockSpec((None, 1, tn), lambda i, j: (i, 0, j)),
        ],
        out_specs=pl.BlockSpec((None, nrow, tn), lambda i, j: (i, 0, j)),
        compiler_params=pltpu.CompilerParams(vmem_limit_bytes=VMEM_LIMIT),
        name="adaln_mods",
    )(c_all, ada_w, ada_b.reshape(depth, 1, 6 * D_MODEL))


CM_OFF = SSD_DIM + XBC_DIM
DT_OFF = CM_OFF + 2 * CM_DIM
MM_N = 256


def _inproj_kernel(x_ref, mod_ref, g_ref, w_ref, z_ref, xbc_ref, u_ref, dt_ref):
    x = x_ref[...]
    ms = jnp.mean(x * x, axis=-1, keepdims=True)
    y = x * lax.rsqrt(ms + EPS) * g_ref[...]
    sh1 = mod_ref[:, 0:D_MODEL]
    s1 = mod_ref[:, D_MODEL:2 * D_MODEL]
    h = (y * (1.0 + s1) + sh1).astype(BF16)
    step = 512
    for c0 in range(0, SSD_DIM, step):
        z_ref[:, c0:c0 + step] = _dot(h, w_ref[:, c0:c0 + step]).astype(z_ref.dtype)
    for c0 in range(0, XBC_DIM, step):
        xbc_ref[:, c0:c0 + step] = (
            _dot(h, w_ref[:, SSD_DIM + c0:SSD_DIM + c0 + step]).astype(xbc_ref.dtype))
    for c0 in range(0, CM_DIM, step):
        a = _dot(h, w_ref[:, CM_OFF + c0:CM_OFF + c0 + step])
        gate = _dot(h, w_ref[:, CM_OFF + CM_DIM + c0:CM_OFF + CM_DIM + c0 + step])
        u_ref[:, c0:c0 + step] = (a * jax.nn.sigmoid(gate)).astype(u_ref.dtype)
    dt_ref[...] = _dot(h, w_ref[:, DT_OFF:DT_OFF + DT_PAD])


def _inproj(xall, mods, norm_g, w_in_b, layer, n_lat_rows, lat_len):
    rows = xall.shape[0]
    tm = TM_IN
    n_lat_tiles = n_lat_rows // tm
    tiles_per_batch = lat_len // tm
    n_batch = n_lat_rows // lat_len

    def mod_idx(t):
        return jnp.where(t < n_lat_tiles, t // tiles_per_batch, n_batch)

    const3 = lambda t: (layer, 0, 0)
    row_spec = lambda width: pl.BlockSpec((tm, width), lambda t: (t, 0))
    return pl.pallas_call(
        _inproj_kernel,
        out_shape=(
            jax.ShapeDtypeStruct((rows, SSD_DIM), BF16),
            jax.ShapeDtypeStruct((rows, XBC_DIM), BF16),
            jax.ShapeDtypeStruct((rows, CM_DIM), BF16),
            jax.ShapeDtypeStruct((rows, DT_PAD), F32),
        ),
        grid=(rows // tm,),
        in_specs=[
            row_spec(D_MODEL),
            pl.BlockSpec((None, None, 1, 6 * D_MODEL), lambda t: (layer, mod_idx(t), 0, 0)),
            pl.BlockSpec((None, 1, D_MODEL), const3),
            pl.BlockSpec((None, D_MODEL, IN_COLS), const3, pipeline_mode=pl.Buffered(1)),
        ],
        out_specs=(row_spec(SSD_DIM), row_spec(XBC_DIM), row_spec(CM_DIM), row_spec(DT_PAD)),
        compiler_params=pltpu.CompilerParams(vmem_limit_bytes=VMEM_LIMIT),
        name="inproj",
    )(xall, mods, norm_g, w_in_b)


def _split_hi_mid_lo(v):
    hi = v.astype(BF16)
    r = v - hi.astype(F32)
    mid = r.astype(BF16)
    lo = (r - mid.astype(F32)).astype(BF16)
    return hi, mid, lo


def _ssd_kernel(xf_ref, xfp_ref, xfn_ref, xb_ref, xbp_ref, xbn_ref, dtf_ref, dtb_ref,
                cw_ref, cb_ref, dtbias_ref, alog_ref, dskip_ref, exp_ref,
                yf_ref, yb_ref, ext_ref, cache_ref, sf_ref, sb_ref, *, n_ctx, n_lat):
    j = pl.program_id(1)

    @pl.when(j == 0)
    def _():
        sf_ref[...] = jnp.zeros_like(sf_ref)
        sb_ref[...] = jnp.zeros_like(sb_ref)

    is_ctx = j < n_ctx
    nchunks = jnp.where(is_ctx, n_ctx, n_lat)
    cf = jnp.where(is_ctx, j, j - n_ctx)
    cbk = nchunks - 1 - cf
    seq_base = jnp.where(is_ctx, 0, n_ctx)
    slot_f = seq_base + cf
    slot_b = seq_base + cbk
    fwd_first = 2 * cf <= nchunks - 1
    bwd_first = 2 * cf < nchunks - 1

    ri = lax.broadcasted_iota(jnp.int32, (CHUNK, CHUNK), 0)
    ci = lax.broadcasted_iota(jnp.int32, (CHUNK, CHUNK), 1)
    low_mask = ri >= ci
    up_mask = ri <= ci
    lane_lo = ci < SSD_HEAD_DIM
    a_neg = -jnp.exp(alog_ref[...])

    def conv_act(x_ref, xp_ref, xn_ref, cidx, act_ref):
        prev = xp_ref[...].astype(F32)[HALO - 8:HALO]
        nxt = xn_ref[...].astype(F32)[0:8]
        ext_ref[0:8, :] = jnp.where(cidx > 0, prev, 0.0)
        ext_ref[8:8 + CHUNK, :] = x_ref[...].astype(F32)
        ext_ref[8 + CHUNK:16 + CHUNK, :] = jnp.where(cidx < nchunks - 1, nxt, 0.0)
        for lb in range(XBC_DIM // LANES):
            cols = slice(lb * LANES, (lb + 1) * LANES)
            acc = jnp.broadcast_to(cb_ref[:, cols], (CHUNK, LANES))
            for k in range(SSD_CONV_W):
                acc = acc + cw_ref[k:k + 1, cols] * ext_ref[6 + k:6 + k + CHUNK, cols]
            act_ref[:, cols] = _silu(acc)

    def direction(dt_ref, col_off, forward, s_ref, y_ref, act_ref):
        dtv = jax.nn.softplus(dt_ref[...] + dtbias_ref[...])
        dta = dtv * (a_neg * LOG2_E)
        tri = jnp.where(low_mask if forward else up_mask, 1.0, 0.0).astype(BF16)
        hi, mid, lo = _split_hi_mid_lo(dta)
        acs = _dot(tri, hi) + _dot(tri, mid) + _dot(tri, lo)
        used = slice(0, 2 * SSD_HEADS)
        acs_row = acs.T[used] - jnp.log2(dtv.T[used])
        edge = acs[CHUNK - 1:CHUNK, :] if forward else acs[0:1, :]
        wst = dtv * jnp.exp2(edge - acs)
        cd16 = jnp.broadcast_to(jnp.exp2(edge), (16, LANES))
        v = jnp.concatenate([wst, cd16], axis=0)
        v_hi = v.astype(BF16)
        v_mid = (v - v_hi.astype(F32)).astype(BF16)
        expanded = _dot(jnp.concatenate([v_hi, v_mid], axis=1),
                        exp_ref[0 if forward else 1])
        w_exp = expanded[0:CHUNK]
        cd_exp = expanded[CHUNK:CHUNK + 1]
        yield

        mask = low_mask if forward else up_mask
        for g in range(SSD_GROUPS):
            b_g = act_ref[:, SSD_DIM + g * SSD_STATE:SSD_DIM + (g + 1) * SSD_STATE].astype(BF16)
            c_off = SSD_DIM + SSD_GROUPS * SSD_STATE + g * SSD_STATE
            c_f32 = act_ref[:, c_off:c_off + SSD_STATE]
            gmat = lax.dot_general(c_f32.astype(BF16), b_g, (((1,), (1,)), ((), ())),
                                   preferred_element_type=F32)
            heads_per_group = SSD_HEADS // SSD_GROUPS
            for pair in range(heads_per_group // 2):
                h0 = g * heads_per_group + 2 * pair
                cols = slice(h0 * SSD_HEAD_DIM, (h0 + 2) * SSD_HEAD_DIM)
                xs_pair = act_ref[:, cols]
                rhs = jnp.concatenate([xs_pair.astype(BF16), s_ref[:, cols].astype(BF16)], axis=0)
                res = []
                for h in (h0, h0 + 1):
                    c = col_off + h
                    col = jnp.broadcast_to(acs[:, c:c + 1], (CHUNK, CHUNK))
                    seg = col - acs_row[c:c + 1, :]
                    dec = jnp.exp2(jnp.where(mask, seg, -jnp.inf))
                    sc = (gmat * dec).astype(BF16)
                    ce = (c_f32 * jnp.exp2(col)).astype(BF16)
                    res.append(_dot(jnp.concatenate([sc, ce], axis=1), rhs))
                y = jnp.where(lane_lo, res[0], res[1])
                if forward:
                    y = y + dskip_ref[:, cols] * xs_pair
                y_ref[:, cols] = y
                yield
            gcols = slice(g * heads_per_group * SSD_HEAD_DIM, (g + 1) * heads_per_group * SSD_HEAD_DIM)
            xw = (act_ref[:, gcols] * w_exp[:, gcols]).astype(BF16)
            upd = lax.dot_general(b_g, xw, (((0,), (0,)), ((), ())), preferred_element_type=F32)
            s_ref[:, gcols] = s_ref[:, gcols] * cd_exp[:, gcols] + upd
            yield

    @pl.when(fwd_first)
    def _():
        conv_act(xf_ref, xfp_ref, xfn_ref, cf, cache_ref.at[slot_f])

    @pl.when(bwd_first)
    def _():
        conv_act(xb_ref, xbp_ref, xbn_ref, cbk, cache_ref.at[slot_b])

    fwd = direction(dtf_ref, 0, True, sf_ref, yf_ref, cache_ref.at[slot_f])
    bwd = direction(dtb_ref, SSD_HEADS, False, sb_ref, yb_ref, cache_ref.at[slot_b])
    for _ in zip(fwd, bwd):
        pass


def _ssd(xbc, dt, conv_w, conv_b, dt_bias, a_log, d_skip_exp, expand, layer, n_batch, lat_len, ctx_len):
    rows = xbc.shape[0]
    n_lat = lat_len // CHUNK
    n_ctx = ctx_len // CHUNK
    ctx0 = n_batch * n_lat
    last16 = rows // HALO - 1
    per16 = CHUNK // HALO

    def fwd_blk(b, j):
        return jnp.where(j < n_ctx, ctx0 + b * n_ctx + j, b * n_lat + (j - n_ctx))

    def bwd_blk(b, j):
        return jnp.where(j < n_ctx, ctx0 + b * n_ctx + (n_ctx - 1 - j),
                         b * n_lat + (n_lat - 1 - (j - n_ctx)))

    def main(blk):
        return lambda b, j: (blk(b, j), 0)

    def prev(blk):
        return lambda b, j: (jnp.maximum(blk(b, j) * per16 - 1, 0), 0)

    def nxt(blk):
        return lambda b, j: (jnp.minimum(blk(b, j) * per16 + per16, last16), 0)

    const3 = lambda b, j: (layer, 0, 0)
    x_specs = []
    for blk in (fwd_blk, bwd_blk):
        x_specs += [pl.BlockSpec((CHUNK, XBC_DIM), main(blk)),
                    pl.BlockSpec((HALO, XBC_DIM), prev(blk)),
                    pl.BlockSpec((HALO, XBC_DIM), nxt(blk))]
    kern = functools.partial(_ssd_kernel, n_ctx=n_ctx, n_lat=n_lat)
    return pl.pallas_call(
        kern,
        out_shape=(jax.ShapeDtypeStruct((rows, SSD_DIM), F32),
                   jax.ShapeDtypeStruct((rows, SSD_DIM), F32)),
        grid=(n_batch, n_ctx + n_lat),
        in_specs=x_specs + [
            pl.BlockSpec((CHUNK, DT_PAD), main(fwd_blk)),
            pl.BlockSpec((CHUNK, DT_PAD), main(bwd_blk)),
            pl.BlockSpec((None, SSD_CONV_W, XBC_DIM), const3),
            pl.BlockSpec((None, 1, XBC_DIM), const3),
            pl.BlockSpec((None, 1, DT_PAD), const3),
            pl.BlockSpec((None, 1, DT_PAD), const3),
            pl.BlockSpec((None, 1, SSD_DIM), const3),
            pl.BlockSpec((2, 2 * LANES, SSD_DIM), lambda b, j: (0, 0, 0)),
        ],
        out_specs=(pl.BlockSpec((CHUNK, SSD_DIM), main(fwd_blk)),
                   pl.BlockSpec((CHUNK, SSD_DIM), main(bwd_blk))),
        scratch_shapes=[
            pltpu.VMEM((CHUNK + 16, XBC_DIM), F32),
            pltpu.VMEM((n_ctx + n_lat, CHUNK, XBC_DIM), F32),
            pltpu.VMEM((SSD_STATE, SSD_DIM), F32),
            pltpu.VMEM((SSD_STATE, SSD_DIM), F32),
        ],
        compiler_params=pltpu.CompilerParams(
            dimension_semantics=("arbitrary", "arbitrary"), vmem_limit_bytes=VMEM_LIMIT),
        name="ssd_scan",
    )(xbc, xbc, xbc, xbc, xbc, xbc, dt, dt, conv_w, conv_b, dt_bias, a_log, d_skip_exp, expand)


def _mixffn_kernel(x_ref, yf_ref, yb_ref, z_ref, u_ref, mod_ref, ng_ref, n2g_ref, fg_ref,
                   tap_ref, cb_ref, lng_ref, lnb_ref, wo_ref, w1_ref, w2_ref,
                   o_ref, act_ref, x1_ref, sh_ref, conv_ref, cmo_ref, *, final, n_tiles, n_lat_tiles,
                   ctx_len):
    t = pl.program_id(0)
    tm = x_ref.shape[0]
    half = CM_KERNEL // 2
    pad = CM_PAD
    assert pad - half + CM_KERNEL == CM_TAP_ROWS and CM_TAP_ROWS % SUBLANES == 0
    blk_rows = CM_BLK_ROWS * SUBLANES
    blk_lanes = CM_BLK_TILES * LANES
    lane_groups = CM_DIM // blk_lanes
    n_blocks = (tm // blk_rows) * lane_groups
    cur = t % 2
    prev = 1 - cur

    @pl.when(t == 0)
    def _():
        cmo_ref[1] = jnp.zeros(cmo_ref.shape[1:], cmo_ref.dtype)

    def body(seg_len):
        nseg = tm // seg_len
        stride = seg_len + 2 * pad
        assert stride % SUBLANES == 0 and seg_len % blk_rows == 0
        zeros = jnp.zeros((pad, CM_DIM), F32)
        span = nseg * stride - SUBLANES

        def conv_block(i):
            lane0 = (i % lane_groups) * blk_lanes
            row0 = (i // lane_groups) * blk_rows
            seg = row0 // seg_len
            origin0 = seg * stride + (row0 - seg * seg_len)
            lanes = [slice(lane0 + k * LANES, lane0 + (k + 1) * LANES) for k in range(CM_BLK_TILES)]
            acc = [jnp.broadcast_to(cb_ref[:, c], (SUBLANES, LANES)) for c in lanes] * CM_BLK_ROWS
            for off in range(CM_TAP_ROWS):
                r, q8 = off % SUBLANES, off - off % SUBLANES
                wk = [jnp.broadcast_to(tap_ref[off, :, c], (SUBLANES, LANES)) for c in lanes]
                for v in range(CM_BLK_ROWS):
                    first = origin0 + v * SUBLANES + q8
                    for k, c in enumerate(lanes):
                        acc[v * CM_BLK_TILES + k] = (
                            acc[v * CM_BLK_TILES + k] + wk[k] * sh_ref[r, first:first + SUBLANES, c])
            for v in range(CM_BLK_ROWS):
                for k, c in enumerate(lanes):
                    conv_ref[row0 + v * SUBLANES:row0 + (v + 1) * SUBLANES, c] = acc[v * CM_BLK_TILES + k]
            return functools.reduce(lambda p, q: p + q, acc)

        for s in range(nseg):
            sh_ref[0, s * stride:s * stride + pad, :] = zeros
            sh_ref[0, s * stride + pad:s * stride + pad + seg_len, :] = (
                u_ref[s * seg_len:(s + 1) * seg_len, :].astype(F32))
            sh_ref[0, s * stride + pad + seg_len:(s + 1) * stride, :] = zeros
        for lt0 in range(0, CM_DIM // LANES, CM_BLK_TILES):
            cols = slice(lt0 * LANES, (lt0 + CM_BLK_TILES) * LANES)
            for r in range(1, SUBLANES):
                sh_ref[r, 0:span, cols] = sh_ref[0, r:r + span, cols]

        mod = lambda i: mod_ref[:, i * D_MODEL:(i + 1) * D_MODEL]
        v = (yf_ref[...] + yb_ref[...]) * _silu(z_ref[...].astype(F32))
        ms = jnp.mean(v * v, axis=-1, keepdims=True)
        so = (v * lax.rsqrt(ms + EPS) * ng_ref[...]).astype(BF16)
        n_chunks = FFN_HIDDEN // MM_N
        n_cols = D_MODEL // MM_N
        extra = n_blocks - n_cols - n_chunks
        per_piece = [1] * n_cols + [2] * extra + [1] * (n_chunks - extra)
        assert 0 <= extra <= n_chunks and sum(per_piece) == n_blocks
        counts = iter(per_piece)
        blocks = iter(range(n_blocks))

        def conv_some():
            zero = jnp.zeros((SUBLANES, LANES), F32)
            for _ in range(next(counts)):
                zero = zero + _sched_zero(conv_block(next(blocks)))
            return zero

        ln_rows = tm // n_cols

        def layer_norm_some(j):
            rows = slice(j * ln_rows, (j + 1) * ln_rows)
            cv = conv_ref[rows, :]
            mu = jnp.mean(cv, axis=-1, keepdims=True)
            xc = cv - mu
            var = jnp.mean(xc * xc, axis=-1, keepdims=True)
            out = _silu(xc * lax.rsqrt(var + EPS) * lng_ref[...] + lnb_ref[...])
            cmo_ref[cur, rows, :] = out.astype(cmo_ref.dtype)
            folded = functools.reduce(
                lambda p, q: p + q,
                [out[r0:r0 + SUBLANES, c0:c0 + LANES]
                 for r0 in range(0, ln_rows, SUBLANES) for c0 in range(0, CM_DIM, LANES)])
            return _sched_zero(folded)

        def anchored_f32(ref, c0, anchor):
            ref[0:SUBLANES, c0:c0 + LANES] = ref[0:SUBLANES, c0:c0 + LANES] + anchor

        anchor = jnp.zeros((SUBLANES, LANES), F32)
        cmo_prev = cmo_ref[prev]
        for j in range(n_cols):
            cols = slice(j * MM_N, (j + 1) * MM_N)
            mix = (_dot(so, wo_ref[0:SSD_DIM, cols])
                   + _dot(cmo_prev, wo_ref[SSD_DIM:SSD_DIM + CM_DIM, cols]))
            x1_ref[:, cols] = x_ref[:, cols] + mod_ref[:, 2 * D_MODEL + j * MM_N:2 * D_MODEL + (j + 1) * MM_N] * mix
            anchored_f32(x1_ref, j * MM_N, anchor)
            anchor = conv_some()
        x1 = x1_ref[...]
        ms2 = jnp.mean(x1 * x1, axis=-1, keepdims=True)
        h2 = (x1 * lax.rsqrt(ms2 + EPS) * n2g_ref[...] * (1.0 + mod(4)) + mod(3)).astype(BF16)
        top = 2 * SUBLANES
        for ci in range(n_chunks):
            c0 = ci * MM_N
            a = _dot(h2, w1_ref[:, c0:c0 + MM_N])
            b = _dot(h2, w1_ref[:, FFN_HIDDEN + c0:FFN_HIDDEN + c0 + MM_N])
            hs = _silu(a) * b
            z16 = jnp.concatenate([anchor, anchor], axis=0)
            act_ref[0:top, c0:c0 + MM_N] = (
                hs[0:top] + jnp.concatenate([z16] * (MM_N // LANES), axis=1)).astype(BF16)
            act_ref[top:, c0:c0 + MM_N] = hs[top:].astype(BF16)
            anchor = conv_some()
        for j in range(n_cols):
            cols = slice(j * MM_N, (j + 1) * MM_N)
            g2 = mod_ref[:, 5 * D_MODEL + j * MM_N:5 * D_MODEL + (j + 1) * MM_N]
            o_ref[:, cols] = x1_ref[:, cols] + g2 * _dot(act_ref[...], w2_ref[:, cols])
            anchored_f32(o_ref, j * MM_N, anchor)
            anchor = layer_norm_some(j)
        anchored_f32(o_ref, 0, anchor)
        if final:
            x2 = o_ref[...]
            msf = jnp.mean(x2 * x2, axis=-1, keepdims=True)
            o_ref[...] = x2 * lax.rsqrt(msf + EPS) * fg_ref[...]

    conv_tile = jnp.minimum(t, n_tiles - 1)

    @pl.when(conv_tile < n_lat_tiles)
    def _():
        body(GRID_W)

    @pl.when(conv_tile >= n_lat_tiles)
    def _():
        body(ctx_len)


def _mixffn(xall, yf, yb, z, u, mods, ssd_norm_g, norm2_g, final_g, cm_w_rows, cm_b, ln_g, ln_b,
            w_out_b, w1_b, w2_b, layer, n_lat_rows, lat_len, ctx_len, final):
    tm = TM_MIX
    assert tm == ctx_len and tm % GRID_W == 0
    rows = n_lat_rows if final else xall.shape[0]
    n_tiles = rows // tm
    n_lat_tiles = n_lat_rows // tm
    tiles_per_batch = lat_len // tm
    n_batch = n_lat_rows // lat_len

    ffn_tile = lambda t: jnp.maximum(t - 1, 0)
    conv_tile = lambda t: jnp.minimum(t, n_tiles - 1)

    def mod_idx(t):
        return jnp.where(ffn_tile(t) < n_lat_tiles, ffn_tile(t) // tiles_per_batch, n_batch)

    ffn_spec = lambda width: pl.BlockSpec((tm, width), lambda t: (ffn_tile(t), 0))
    const3 = lambda t: (layer, 0, 0)
    single = pl.Buffered(1)
    kern = functools.partial(_mixffn_kernel, final=final, n_tiles=n_tiles, n_lat_tiles=n_lat_tiles,
                             ctx_len=ctx_len)
    pad_rows = max((tm // GRID_W) * (GRID_W + 2 * CM_PAD), ctx_len + 2 * CM_PAD)
    return pl.pallas_call(
        kern,
        out_shape=jax.ShapeDtypeStruct((rows, D_MODEL), F32),
        grid=(n_tiles + 1,),
        in_specs=[
            ffn_spec(D_MODEL), ffn_spec(SSD_DIM), ffn_spec(SSD_DIM), ffn_spec(SSD_DIM),
            pl.BlockSpec((tm, CM_DIM), lambda t: (conv_tile(t), 0)),
            pl.BlockSpec((None, None, 1, 6 * D_MODEL), lambda t: (layer, mod_idx(t), 0, 0)),
            pl.BlockSpec((None, 1, SSD_DIM), const3),
            pl.BlockSpec((None, 1, D_MODEL), const3),
            pl.BlockSpec((1, D_MODEL), lambda t: (0, 0)),
            pl.BlockSpec((None, CM_TAP_ROWS, 1, CM_DIM), lambda t: (layer, 0, 0, 0)),
            pl.BlockSpec((None, 1, CM_DIM), const3),
            pl.BlockSpec((None, 1, CM_DIM), const3),
            pl.BlockSpec((None, 1, CM_DIM), const3),
            pl.BlockSpec((None, SSD_DIM + CM_DIM, D_MODEL), const3, pipeline_mode=single),
            pl.BlockSpec((None, D_MODEL, 2 * FFN_HIDDEN), const3, pipeline_mode=single),
            pl.BlockSpec((None, FFN_HIDDEN, D_MODEL), const3, pipeline_mode=single),
        ],
        out_specs=ffn_spec(D_MODEL),
        scratch_shapes=[pltpu.VMEM((tm, FFN_HIDDEN), BF16),
                        pltpu.VMEM((tm, D_MODEL), F32),
                        pltpu.VMEM((SUBLANES, pad_rows, CM_DIM), F32),
                        pltpu.VMEM((tm, CM_DIM), F32),
                        pltpu.VMEM((2, tm, CM_DIM), BF16)],
        compiler_params=pltpu.CompilerParams(
            dimension_semantics=("arbitrary",), vmem_limit_bytes=VMEM_LIMIT),
        name="mix_ffn",
    )(xall, yf, yb, z, u, mods, ssd_norm_g, norm2_g, final_g, cm_w_rows, cm_b, ln_g, ln_b,
      w_out_b, w1_b, w2_b)


def kernel(x, c, ctx, c_ctx, w_in, ssd_conv_w, ssd_conv_b, dt_bias, a_log, d_skip, ssd_norm_g,
           cm_dw_w, cm_dw_b, cm_ln_g, cm_ln_b, w_out, w_ffn_in, w_ffn_out, ada_w, ada_b,
           norm1_g, norm2_g, final_norm_g):
    n_batch, lat_len, _ = x.shape
    ctx_len = ctx.shape[1]
    depth = w_in.shape[0]
    n_lat_rows = n_batch * lat_len
    assert lat_len % TM_IN == 0 and (n_batch * ctx_len) % TM_IN == 0
    assert lat_len % CHUNK == 0 and ctx_len % CHUNK == 0 and n_batch + 1 <= 16

    xall = jnp.concatenate([x.reshape(n_lat_rows, D_MODEL),
                            ctx.reshape(n_batch * ctx_len, D_MODEL)], axis=0)

    c_all = jnp.concatenate([c, c_ctx[None, :],
                             jnp.zeros((16 - n_batch - 1, D_MODEL), F32)], axis=0)
    mods = _mods(c_all, ada_w, ada_b).reshape(depth, 16, 1, 6 * D_MODEL)

    i0, i1, i2 = SSD_DIM, SSD_DIM + XBC_DIM, SSD_DIM + XBC_DIM + 2 * SSD_HEADS
    w_in_b = jnp.concatenate(
        [w_in[..., :i1], w_in[..., i2:], w_in[..., i1:i2],
         jnp.zeros((depth, D_MODEL, DT_PAD - 2 * SSD_HEADS), F32)], axis=-1).astype(BF16)
    w_out_b = w_out.astype(BF16)
    w1_b = w_ffn_in.astype(BF16)
    w2_b = w_ffn_out.astype(BF16)
    pad_lanes = lambda v: jnp.pad(v.reshape(depth, 1, 2 * SSD_HEADS),
                                  ((0, 0), (0, 0), (0, DT_PAD - 2 * SSD_HEADS)))
    dt_bias_p = pad_lanes(dt_bias)
    a_log_p = pad_lanes(a_log)
    d_skip_exp = jnp.repeat(d_skip, SSD_HEAD_DIM, axis=-1).reshape(depth, 1, SSD_DIM)
    row3 = lambda v: v.reshape(depth, 1, v.shape[-1])
    cm_w_rows = jnp.pad(cm_dw_w, ((0, 0), (CM_TAP_ROWS - CM_KERNEL, 0), (0, 0))).reshape(
        depth, CM_TAP_ROWS, 1, CM_DIM)

    r = jnp.arange(2 * LANES)[:, None] % LANES
    col_head = jnp.arange(SSD_DIM)[None, :] // SSD_HEAD_DIM
    expand = jnp.stack([(r == col_head), (r == col_head + SSD_HEADS)]).astype(BF16)

    for i in range(depth):
        final = i == depth - 1
        z, xbc, u, dt = _inproj(xall, mods, row3(norm1_g), w_in_b, i, n_lat_rows, lat_len)
        yf, yb = _ssd(xbc, dt, ssd_conv_w, row3(ssd_conv_b), dt_bias_p, a_log_p, d_skip_exp,
                      expand, i, n_batch, lat_len, ctx_len)
        xall = _mixffn(xall, yf, yb, z, u, mods, row3(ssd_norm_g), row3(norm2_g),
                       final_norm_g.reshape(1, D_MODEL), cm_w_rows, row3(cm_dw_b),
                       row3(cm_ln_g), row3(cm_ln_b), w_out_b, w1_b, w2_b,
                       i, n_lat_rows, lat_len, ctx_len, final)
    return xall.reshape(n_batch, lat_len, D_MODEL)
```

```python
import functools

import jax
import jax.numpy as jnp
from jax import lax
from jax.experimental import pallas as pl
from jax.experimental.pallas import tpu as pltpu

F32 = jnp.float32
BF16 = jnp.bfloat16

D_MODEL = 1024
SSD_HEADS = 16
SSD_HEAD_DIM = 64
SSD_DIM = SSD_HEADS * SSD_HEAD_DIM
SSD_GROUPS = 2
SSD_STATE = 128
SSD_CONV_W = 5
CHUNK = 128
XBC_DIM = SSD_DIM + 2 * SSD_GROUPS * SSD_STATE
CM_DIM = D_MODEL
CM_KERNEL = 31
GRID_W = 64
FFN_HIDDEN = 2816
EPS = 1e-6
LOG2_E = 1.4426950408889634

LANES = 128
SUBLANES = 8
DT_PAD = LANES
IN_COLS = SSD_DIM + XBC_DIM + 2 * CM_DIM + DT_PAD
HALO = 16
VMEM_LIMIT = 56 * 1024 * 1024

TM_IN = 512
TM_MIX = 256
MIX_FFN_ROWS = 256
MIX_OUTPROJ_BLOCKS = 4
CM_BLK_ROWS = 4
CM_BLK_TILES = 4
CM_PAD = 16
CM_TAP_ROWS = 32


def _silu(v):
    return v * jax.nn.sigmoid(v)


def _dot(a, b):
    return jnp.dot(a, b, preferred_element_type=F32)


def _sched_zero(v):
    bits = pltpu.bitcast(v, jnp.uint32)
    return pltpu.bitcast(lax.shift_right_logical(bits, jnp.uint32(32)), F32)


def _mods_kernel(c_ref, w_ref, b_ref, o_ref):
    sc = _silu(c_ref[...])
    o_ref[...] = _dot(sc.astype(BF16), w_ref[...].astype(BF16)) + b_ref[...]


def _mods(c_all, ada_w, ada_b):
    depth = ada_w.shape[0]
    nrow = c_all.shape[0]
    tn = 1536
    return pl.pallas_call(
        _mods_kernel,
        out_shape=jax.ShapeDtypeStruct((depth, nrow, 6 * D_MODEL), F32),
        grid=(depth, 6 * D_MODEL // tn),
        in_specs=[
            pl.BlockSpec((nrow, D_MODEL), lambda i, j: (0, 0)),
            pl.BlockSpec((None, D_MODEL, tn), lambda i, j: (i, 0, j)),
            pl.BlockSpec((None, 1, tn), lambda i, j: (i, 0, j)),
        ],
        out_specs=pl.BlockSpec((None, nrow, tn), lambda i, j: (i, 0, j)),
        compiler_params=pltpu.CompilerParams(vmem_limit_bytes=VMEM_LIMIT),
        name="adaln_mods",
    )(c_all, ada_w, ada_b.reshape(depth, 1, 6 * D_MODEL))


CM_OFF = SSD_DIM + XBC_DIM
DT_OFF = CM_OFF + 2 * CM_DIM
MM_N = 256


def _inproj_kernel(*refs, n_lat_tiles, split):
    if split:
        xl_ref, xc_ref, mod_ref, g_ref, w_ref, z_ref, xbc_ref, u_ref, dt_ref = refs
        x = jnp.where(pl.program_id(0) < n_lat_tiles, xl_ref[...], xc_ref[...])
    else:
        xl_ref, mod_ref, g_ref, w_ref, z_ref, xbc_ref, u_ref, dt_ref = refs
        x = xl_ref[...]
    ms = jnp.mean(x * x, axis=-1, keepdims=True)
    y = x * lax.rsqrt(ms + EPS) * g_ref[...]
    sh1 = mod_ref[:, 0:D_MODEL]
    s1 = mod_ref[:, D_MODEL:2 * D_MODEL]
    h = (y * (1.0 + s1) + sh1).astype(BF16)
    step = 512
    for c0 in range(0, SSD_DIM, step):
        z_ref[:, c0:c0 + step] = _dot(h, w_ref[:, c0:c0 + step]).astype(z_ref.dtype)
    for c0 in range(0, XBC_DIM, step):
        xbc_ref[:, c0:c0 + step] = (
            _dot(h, w_ref[:, SSD_DIM + c0:SSD_DIM + c0 + step]).astype(xbc_ref.dtype))
    for c0 in range(0, CM_DIM, step):
        a = _dot(h, w_ref[:, CM_OFF + c0:CM_OFF + c0 + step])
        gate = _dot(h, w_ref[:, CM_OFF + CM_DIM + c0:CM_OFF + CM_DIM + c0 + step])
        u_ref[:, c0:c0 + step] = (a * jax.nn.sigmoid(gate)).astype(u_ref.dtype)
    dt_ref[...] = _dot(h, w_ref[:, DT_OFF:DT_OFF + DT_PAD])


def _inproj(x_lat, x_ctx, ctx_row0, rows, mods, norm_g, w_in_b, layer, n_lat_rows, lat_len):
    tm = TM_IN
    n_lat_tiles = n_lat_rows // tm
    tiles_per_batch = lat_len // tm
    n_batch = n_lat_rows // lat_len
    ctx_tile0 = ctx_row0 // tm

    def mod_idx(t):
        return jnp.where(t < n_lat_tiles, t // tiles_per_batch, n_batch)

    const3 = lambda t: (layer, 0, 0)
    row_spec = lambda width: pl.BlockSpec((tm, width), lambda t: (t, 0))
    split = x_ctx is not x_lat
    kern = functools.partial(_inproj_kernel, n_lat_tiles=n_lat_tiles, split=split)
    if split:
        x_specs = [
            pl.BlockSpec((tm, D_MODEL), lambda t: (jnp.minimum(t, n_lat_tiles - 1), 0)),
            pl.BlockSpec((tm, D_MODEL), lambda t: (jnp.maximum(t - n_lat_tiles, 0) + ctx_tile0, 0))]
        x_args = [x_lat, x_ctx]
    else:
        x_specs, x_args = [row_spec(D_MODEL)], [x_lat]
    return pl.pallas_call(
        kern,
        out_shape=(
            jax.ShapeDtypeStruct((rows, SSD_DIM), BF16),
            jax.ShapeDtypeStruct((rows, XBC_DIM), BF16),
            jax.ShapeDtypeStruct((rows, CM_DIM), BF16),
            jax.ShapeDtypeStruct((rows, DT_PAD), F32),
        ),
        grid=(rows // tm,),
        in_specs=x_specs + [
            pl.BlockSpec((None, None, 1, 6 * D_MODEL), lambda t: (layer, mod_idx(t), 0, 0)),
            pl.BlockSpec((None, 1, D_MODEL), const3),
            pl.BlockSpec((None, D_MODEL, IN_COLS), const3, pipeline_mode=pl.Buffered(1)),
        ],
        out_specs=(row_spec(SSD_DIM), row_spec(XBC_DIM), row_spec(CM_DIM), row_spec(DT_PAD)),
        compiler_params=pltpu.CompilerParams(vmem_limit_bytes=VMEM_LIMIT),
        name="inproj",
    )(*x_args, mods, norm_g, w_in_b)


def _split_hi_mid_lo(v):
    hi = v.astype(BF16)
    r = v - hi.astype(F32)
    mid = r.astype(BF16)
    lo = (r - mid.astype(F32)).astype(BF16)
    return hi, mid, lo


def _ssd_kernel(xf_ref, xfp_ref, xfn_ref, xb_ref, xbp_ref, xbn_ref, dtf_ref, dtb_ref,
                cw_ref, cb_ref, dtbias_ref, alog_ref, dskip_ref, exp_ref,
                yf_ref, yb_ref, ext_ref, cache_ref, sf_ref, sb_ref, *, n_ctx, n_lat):
    j = pl.program_id(1)

    @pl.when(j == 0)
    def _():
        sf_ref[...] = jnp.zeros_like(sf_ref)
        sb_ref[...] = jnp.zeros_like(sb_ref)

    is_ctx = j < n_ctx
    nchunks = jnp.where(is_ctx, n_ctx, n_lat)
    cf = jnp.where(is_ctx, j, j - n_ctx)
    cbk = nchunks - 1 - cf
    seq_base = jnp.where(is_ctx, 0, n_ctx)
    slot_f = seq_base + cf
    slot_b = seq_base + cbk
    fwd_first = 2 * cf <= nchunks - 1
    bwd_first = 2 * cf < nchunks - 1

    ri = lax.broadcasted_iota(jnp.int32, (CHUNK, CHUNK), 0)
    ci = lax.broadcasted_iota(jnp.int32, (CHUNK, CHUNK), 1)
    low_mask = ri >= ci
    up_mask = ri <= ci
    lane_lo = ci < SSD_HEAD_DIM
    a_neg = -jnp.exp(alog_ref[...])

    def conv_act(x_ref, xp_ref, xn_ref, cidx, act_ref):
        prev = xp_ref[...].astype(F32)[HALO - 8:HALO]
        nxt = xn_ref[...].astype(F32)[0:8]
        ext_ref[0:8, :] = jnp.where(cidx > 0, prev, 0.0)
        ext_ref[8:8 + CHUNK, :] = x_ref[...].astype(F32)
        ext_ref[8 + CHUNK:16 + CHUNK, :] = jnp.where(cidx < nchunks - 1, nxt, 0.0)
        for lb in range(XBC_DIM // LANES):
            cols = slice(lb * LANES, (lb + 1) * LANES)
            acc = jnp.broadcast_to(cb_ref[:, cols], (CHUNK, LANES))
            for k in range(SSD_CONV_W):
                acc = acc + cw_ref[k:k + 1, cols] * ext_ref[6 + k:6 + k + CHUNK, cols]
            act_ref[:, cols] = _silu(acc)

    def direction(dt_ref, col_off, forward, s_ref, y_ref, act_ref):
        dtv = jax.nn.softplus(dt_ref[...] + dtbias_ref[...])
        dta = dtv * (a_neg * LOG2_E)
        tri = jnp.where(low_mask if forward else up_mask, 1.0, 0.0).astype(BF16)
        hi, mid, lo = _split_hi_mid_lo(dta)
        acs = _dot(tri, hi) + _dot(tri, mid) + _dot(tri, lo)
        used = slice(0, 2 * SSD_HEADS)
        acs_row = acs.T[used] - jnp.log2(dtv.T[used])
        edge = acs[CHUNK - 1:CHUNK, :] if forward else acs[0:1, :]
        wst = dtv * jnp.exp2(edge - acs)
        cd16 = jnp.broadcast_to(jnp.exp2(edge), (16, LANES))
        v = jnp.concatenate([wst, cd16], axis=0)
        v_hi = v.astype(BF16)
        v_mid = (v - v_hi.astype(F32)).astype(BF16)
        expanded = _dot(jnp.concatenate([v_hi, v_mid], axis=1),
                        exp_ref[0 if forward else 1])
        w_exp = expanded[0:CHUNK]
        cd_exp = expanded[CHUNK:CHUNK + 1]
        yield

        mask = low_mask if forward else up_mask
        for g in range(SSD_GROUPS):
            b_g = act_ref[:, SSD_DIM + g * SSD_STATE:SSD_DIM + (g + 1) * SSD_STATE].astype(BF16)
            c_off = SSD_DIM + SSD_GROUPS * SSD_STATE + g * SSD_STATE
            c_f32 = act_ref[:, c_off:c_off + SSD_STATE]
            gmat = lax.dot_general(c_f32.astype(BF16), b_g, (((1,), (1,)), ((), ())),
                                   preferred_element_type=F32)
            heads_per_group = SSD_HEADS // SSD_GROUPS
            for pair in range(heads_per_group // 2):
                h0 = g * heads_per_group + 2 * pair
                cols = slice(h0 * SSD_HEAD_DIM, (h0 + 2) * SSD_HEAD_DIM)
                xs_pair = act_ref[:, cols]
                rhs = jnp.concatenate([xs_pair.astype(BF16), s_ref[:, cols].astype(BF16)], axis=0)
                res = []
                for h in (h0, h0 + 1):
                    c = col_off + h
                    col = jnp.broadcast_to(acs[:, c:c + 1], (CHUNK, CHUNK))
                    seg = col - acs_row[c:c + 1, :]
                    dec = jnp.exp2(jnp.where(mask, seg, -jnp.inf))
                    sc = (gmat * dec).astype(BF16)
                    ce = (c_f32 * jnp.exp2(col)).astype(BF16)
                    res.append(_dot(jnp.concatenate([sc, ce], axis=1), rhs))
                y = jnp.where(lane_lo, res[0], res[1])
                if forward:
                    y = y + dskip_ref[:, cols] * xs_pair
                y_ref[:, cols] = y
                yield
            gcols = slice(g * heads_per_group * SSD_HEAD_DIM, (g + 1) * heads_per_group * SSD_HEAD_DIM)
            xw = (act_ref[:, gcols] * w_exp[:, gcols]).astype(BF16)
            upd = lax.dot_general(b_g, xw, (((0,), (0,)), ((), ())), preferred_element_type=F32)
            s_ref[:, gcols] = s_ref[:, gcols] * cd_exp[:, gcols] + upd
            yield

    @pl.when(fwd_first)
    def _():
        conv_act(xf_ref, xfp_ref, xfn_ref, cf, cache_ref.at[slot_f])

    @pl.when(bwd_first)
    def _():
        conv_act(xb_ref, xbp_ref, xbn_ref, cbk, cache_ref.at[slot_b])

    fwd = direction(dtf_ref, 0, True, sf_ref, yf_ref, cache_ref.at[slot_f])
    bwd = direction(dtb_ref, SSD_HEADS, False, sb_ref, yb_ref, cache_ref.at[slot_b])
    for _ in zip(fwd, bwd):
        pass


def _ssd(xbc, dt, conv_w, conv_b, dt_bias, a_log, d_skip_exp, expand, layer, n_batch, lat_len, ctx_len):
    rows = xbc.shape[0]
    n_lat = lat_len // CHUNK
    n_ctx = ctx_len // CHUNK
    ctx0 = n_batch * n_lat
    last16 = rows // HALO - 1
    per16 = CHUNK // HALO

    def fwd_blk(b, j):
        return jnp.where(j < n_ctx, ctx0 + b * n_ctx + j, b * n_lat + (j - n_ctx))

    def bwd_blk(b, j):
        return jnp.where(j < n_ctx, ctx0 + b * n_ctx + (n_ctx - 1 - j),
                         b * n_lat + (n_lat - 1 - (j - n_ctx)))

    def main(blk):
        return lambda b, j: (blk(b, j), 0)

    def prev(blk):
        return lambda b, j: (jnp.maximum(blk(b, j) * per16 - 1, 0), 0)

    def nxt(blk):
        return lambda b, j: (jnp.minimum(blk(b, j) * per16 + per16, last16), 0)

    const3 = lambda b, j: (layer, 0, 0)
    x_specs = []
    for blk in (fwd_blk, bwd_blk):
        x_specs += [pl.BlockSpec((CHUNK, XBC_DIM), main(blk)),
                    pl.BlockSpec((HALO, XBC_DIM), prev(blk)),
                    pl.BlockSpec((HALO, XBC_DIM), nxt(blk))]
    kern = functools.partial(_ssd_kernel, n_ctx=n_ctx, n_lat=n_lat)
    return pl.pallas_call(
        kern,
        out_shape=(jax.ShapeDtypeStruct((rows, SSD_DIM), F32),
                   jax.ShapeDtypeStruct((rows, SSD_DIM), F32)),
        grid=(n_batch, n_ctx + n_lat),
        in_specs=x_specs + [
            pl.BlockSpec((CHUNK, DT_PAD), main(fwd_blk)),
            pl.BlockSpec((CHUNK, DT_PAD), main(bwd_blk)),
            pl.BlockSpec((None, SSD_CONV_W, XBC_DIM), const3),
            pl.BlockSpec((None, 1, XBC_DIM), const3),
            pl.BlockSpec((None, 1, DT_PAD), const3),
            pl.BlockSpec((None, 1, DT_PAD), const3),
            pl.BlockSpec((None, 1, SSD_DIM), const3),
            pl.BlockSpec((2, 2 * LANES, SSD_DIM), lambda b, j: (0, 0, 0)),
        ],
        out_specs=(pl.BlockSpec((CHUNK, SSD_DIM), main(fwd_blk)),
                   pl.BlockSpec((CHUNK, SSD_DIM), main(bwd_blk))),
        scratch_shapes=[
            pltpu.VMEM((CHUNK + 16, XBC_DIM), F32),
            pltpu.VMEM((n_ctx + n_lat, CHUNK, XBC_DIM), F32),
            pltpu.VMEM((SSD_STATE, SSD_DIM), F32),
            pltpu.VMEM((SSD_STATE, SSD_DIM), F32),
        ],
        compiler_params=pltpu.CompilerParams(
            dimension_semantics=("arbitrary", "arbitrary"), vmem_limit_bytes=VMEM_LIMIT),
        name="ssd_scan",
    )(xbc, xbc, xbc, xbc, xbc, xbc, dt, dt, conv_w, conv_b, dt_bias, a_log, d_skip_exp, expand)


def _mixffn_kernel(*refs, final, n_tiles, n_lat_tiles, ctx_len, split):
    if split:
        xl_ref, xc_ref = refs[:2]
    else:
        xl_ref = xc_ref = refs[0]
    (yf_ref, yb_ref, z_ref, u_ref, mod_ref, ng_ref, n2g_ref, fg_ref,
     tap_ref, cb_ref, lng_ref, lnb_ref, wo_ref, w1_ref, w2_ref,
     o_ref, act_ref, x1_ref, sh_ref, conv_ref, cmo_ref) = refs[2 if split else 1:]
    t = pl.program_id(0)
    tm = xl_ref.shape[0]
    ffn_is_lat = jnp.maximum(t - 1, 0) < n_lat_tiles
    half = CM_KERNEL // 2
    pad = CM_PAD
    assert pad - half + CM_KERNEL == CM_TAP_ROWS and CM_TAP_ROWS % SUBLANES == 0
    blk_rows = CM_BLK_ROWS * SUBLANES
    blk_lanes = CM_BLK_TILES * LANES
    lane_groups = CM_DIM // blk_lanes
    n_blocks = (tm // blk_rows) * lane_groups
    cur = t % 2
    prev = 1 - cur

    @pl.when(t == 0)
    def _():
        cmo_ref[1] = jnp.zeros(cmo_ref.shape[1:], cmo_ref.dtype)

    def body(seg_len):
        nseg = tm // seg_len
        stride = seg_len + 2 * pad
        assert stride % SUBLANES == 0 and seg_len % blk_rows == 0
        zeros = jnp.zeros((pad, CM_DIM), F32)
        span = nseg * stride - SUBLANES

        def conv_block(i):
            lane0 = (i % lane_groups) * blk_lanes
            row0 = (i // lane_groups) * blk_rows
            seg = row0 // seg_len
            origin0 = seg * stride + (row0 - seg * seg_len)
            lanes = [slice(lane0 + k * LANES, lane0 + (k + 1) * LANES) for k in range(CM_BLK_TILES)]
            acc = [jnp.broadcast_to(cb_ref[:, c], (SUBLANES, LANES)) for c in lanes] * CM_BLK_ROWS
            for off in range(CM_TAP_ROWS):
                r, q8 = off % SUBLANES, off - off % SUBLANES
                wk = [jnp.broadcast_to(tap_ref[off, :, c], (SUBLANES, LANES)) for c in lanes]
                for v in range(CM_BLK_ROWS):
                    first = origin0 + v * SUBLANES + q8
                    for k, c in enumerate(lanes):
                        acc[v * CM_BLK_TILES + k] = (
                            acc[v * CM_BLK_TILES + k] + wk[k] * sh_ref[r, first:first + SUBLANES, c])
            for v in range(CM_BLK_ROWS):
                for k, c in enumerate(lanes):
                    conv_ref[row0 + v * SUBLANES:row0 + (v + 1) * SUBLANES, c] = acc[v * CM_BLK_TILES + k]
            return functools.reduce(lambda p, q: p + q, acc)

        for s in range(nseg):
            sh_ref[0, s * stride:s * stride + pad, :] = zeros
            sh_ref[0, s * stride + pad:s * stride + pad + seg_len, :] = (
                u_ref[s * seg_len:(s + 1) * seg_len, :].astype(F32))
            sh_ref[0, s * stride + pad + seg_len:(s + 1) * stride, :] = zeros
        for lt0 in range(0, CM_DIM // LANES, CM_BLK_TILES):
            cols = slice(lt0 * LANES, (lt0 + CM_BLK_TILES) * LANES)
            for r in range(1, SUBLANES):
                sh_ref[r, 0:span, cols] = sh_ref[0, r:r + span, cols]

        mod = lambda i: mod_ref[:, i * D_MODEL:(i + 1) * D_MODEL]
        v = (yf_ref[...] + yb_ref[...]) * _silu(z_ref[...].astype(F32))
        ms = jnp.mean(v * v, axis=-1, keepdims=True)
        so = (v * lax.rsqrt(ms + EPS) * ng_ref[...]).astype(BF16)
        n_chunks = FFN_HIDDEN // MM_N
        n_cols = D_MODEL // MM_N
        in_chunks = n_blocks - MIX_OUTPROJ_BLOCKS
        per_piece = ([1] * MIX_OUTPROJ_BLOCKS + [0] * (n_cols - MIX_OUTPROJ_BLOCKS)
                     + [in_chunks // n_chunks + (1 if ci < in_chunks % n_chunks else 0)
                        for ci in range(n_chunks)])
        assert MIX_OUTPROJ_BLOCKS <= n_cols and sum(per_piece) == n_blocks
        counts = iter(per_piece)
        blocks = iter(range(n_blocks))

        def conv_some():
            zero = jnp.zeros((SUBLANES, LANES), F32)
            for _ in range(next(counts)):
                zero = zero + _sched_zero(conv_block(next(blocks)))
            return zero

        ln_rows = tm // n_cols

        def layer_norm_some(j):
            rows = slice(j * ln_rows, (j + 1) * ln_rows)
            cv = conv_ref[rows, :]
            mu = jnp.mean(cv, axis=-1, keepdims=True)
            xc = cv - mu
            var = jnp.mean(xc * xc, axis=-1, keepdims=True)
            out = _silu(xc * lax.rsqrt(var + EPS) * lng_ref[...] + lnb_ref[...])
            cmo_ref[cur, rows, :] = out.astype(cmo_ref.dtype)
            folded = functools.reduce(
                lambda p, q: p + q,
                [out[r0:r0 + SUBLANES, c0:c0 + LANES]
                 for r0 in range(0, ln_rows, SUBLANES) for c0 in range(0, CM_DIM, LANES)])
            return _sched_zero(folded)

        def anchored_f32(ref, c0, anchor):
            ref[0:SUBLANES, c0:c0 + LANES] = ref[0:SUBLANES, c0:c0 + LANES] + anchor

        anchor = jnp.zeros((SUBLANES, LANES), F32)
        cmo_prev = cmo_ref[prev]
        for j in range(n_cols):
            cols = slice(j * MM_N, (j + 1) * MM_N)
            mix = (_dot(so, wo_ref[0:SSD_DIM, cols])
                   + _dot(cmo_prev, wo_ref[SSD_DIM:SSD_DIM + CM_DIM, cols]))
            g1 = mod_ref[:, 2 * D_MODEL + j * MM_N:2 * D_MODEL + (j + 1) * MM_N]
            xres = jnp.where(ffn_is_lat, xl_ref[:, cols], xc_ref[:, cols]) if split else xl_ref[:, cols]
            x1_ref[:, cols] = xres + g1 * mix
            anchored_f32(x1_ref, j * MM_N, anchor)
            anchor = conv_some()
        x1 = x1_ref[...]
        ms2 = jnp.mean(x1 * x1, axis=-1, keepdims=True)
        h2 = (x1 * lax.rsqrt(ms2 + EPS) * n2g_ref[...] * (1.0 + mod(4)) + mod(3)).astype(BF16)
        top = 2 * SUBLANES
        for ci in range(n_chunks):
            c0 = ci * MM_N
            for r0 in range(0, tm, MIX_FFN_ROWS):
                hrows = h2[r0:r0 + MIX_FFN_ROWS]
                a = _dot(hrows, w1_ref[:, c0:c0 + MM_N])
                b = _dot(hrows, w1_ref[:, FFN_HIDDEN + c0:FFN_HIDDEN + c0 + MM_N])
                hs = _silu(a) * b
                if r0 == 0:
                    z16 = jnp.concatenate([anchor, anchor], axis=0)
                    act_ref[0:top, c0:c0 + MM_N] = (
                        hs[0:top] + jnp.concatenate([z16] * (MM_N // LANES), axis=1)).astype(BF16)
                    act_ref[top:MIX_FFN_ROWS, c0:c0 + MM_N] = hs[top:].astype(BF16)
                else:
                    act_ref[r0:r0 + MIX_FFN_ROWS, c0:c0 + MM_N] = hs.astype(BF16)
            anchor = conv_some()
        for j in range(n_cols):
            cols = slice(j * MM_N, (j + 1) * MM_N)
            g2 = mod_ref[:, 5 * D_MODEL + j * MM_N:5 * D_MODEL + (j + 1) * MM_N]
            o_ref[:, cols] = x1_ref[:, cols] + g2 * _dot(act_ref[...], w2_ref[:, cols])
            anchored_f32(o_ref, j * MM_N, anchor)
            anchor = layer_norm_some(j)
        anchored_f32(o_ref, 0, anchor)
        if final:
            x2 = o_ref[...]
            msf = jnp.mean(x2 * x2, axis=-1, keepdims=True)
            o_ref[...] = x2 * lax.rsqrt(msf + EPS) * fg_ref[...]

    conv_tile = jnp.minimum(t, n_tiles - 1)

    @pl.when(conv_tile < n_lat_tiles)
    def _():
        body(GRID_W)

    @pl.when(conv_tile >= n_lat_tiles)
    def _():
        body(ctx_len)


def _mixffn(x_lat, x_ctx, ctx_row0, all_rows, yf, yb, z, u, mods, ssd_norm_g, norm2_g, final_g,
            cm_w_rows, cm_b, ln_g, ln_b, w_out_b, w1_b, w2_b, layer, n_lat_rows, lat_len, ctx_len,
            final):
    tm = TM_MIX
    assert tm == ctx_len and tm % GRID_W == 0
    rows = n_lat_rows if final else all_rows
    n_tiles = rows // tm
    n_lat_tiles = n_lat_rows // tm
    tiles_per_batch = lat_len // tm
    n_batch = n_lat_rows // lat_len
    ctx_tile0 = ctx_row0 // tm

    ffn_tile = lambda t: jnp.maximum(t - 1, 0)
    conv_tile = lambda t: jnp.minimum(t, n_tiles - 1)

    def mod_idx(t):
        return jnp.where(ffn_tile(t) < n_lat_tiles, ffn_tile(t) // tiles_per_batch, n_batch)

    ffn_spec = lambda width: pl.BlockSpec((tm, width), lambda t: (ffn_tile(t), 0))
    const3 = lambda t: (layer, 0, 0)
    single = pl.Buffered(1)
    split = x_ctx is not x_lat
    kern = functools.partial(_mixffn_kernel, final=final, n_tiles=n_tiles, n_lat_tiles=n_lat_tiles,
                             ctx_len=ctx_len, split=split)
    if split:
        x_specs = [
            pl.BlockSpec((tm, D_MODEL), lambda t: (jnp.minimum(ffn_tile(t), n_lat_tiles - 1), 0)),
            pl.BlockSpec((tm, D_MODEL),
                         lambda t: (jnp.maximum(ffn_tile(t) - n_lat_tiles, 0) + ctx_tile0, 0))]
        x_args = [x_lat, x_ctx]
    else:
        x_specs, x_args = [ffn_spec(D_MODEL)], [x_lat]
    pad_rows = max((tm // GRID_W) * (GRID_W + 2 * CM_PAD), ctx_len + 2 * CM_PAD)
    return pl.pallas_call(
        kern,
        out_shape=jax.ShapeDtypeStruct((rows, D_MODEL), F32),
        grid=(n_tiles + 1,),
        in_specs=x_specs + [
            ffn_spec(SSD_DIM), ffn_spec(SSD_DIM), ffn_spec(SSD_DIM),
            pl.BlockSpec((tm, CM_DIM), lambda t: (conv_tile(t), 0)),
            pl.BlockSpec((None, None, 1, 6 * D_MODEL), lambda t: (layer, mod_idx(t), 0, 0)),
            pl.BlockSpec((None, 1, SSD_DIM), const3),
            pl.BlockSpec((None, 1, D_MODEL), const3),
            pl.BlockSpec((1, D_MODEL), lambda t: (0, 0)),
            pl.BlockSpec((None, CM_TAP_ROWS, 1, CM_DIM), lambda t: (layer, 0, 0, 0)),
            pl.BlockSpec((None, 1, CM_DIM), const3),
            pl.BlockSpec((None, 1, CM_DIM), const3),
            pl.BlockSpec((None, 1, CM_DIM), const3),
            pl.BlockSpec((None, SSD_DIM + CM_DIM, D_MODEL), const3, pipeline_mode=single),
            pl.BlockSpec((None, D_MODEL, 2 * FFN_HIDDEN), const3, pipeline_mode=single),
            pl.BlockSpec((None, FFN_HIDDEN, D_MODEL), const3, pipeline_mode=single),
        ],
        out_specs=ffn_spec(D_MODEL),
        scratch_shapes=[pltpu.VMEM((tm, FFN_HIDDEN), BF16),
                        pltpu.VMEM((tm, D_MODEL), F32),
                        pltpu.VMEM((SUBLANES, pad_rows, CM_DIM), F32),
                        pltpu.VMEM((tm, CM_DIM), F32),
                        pltpu.VMEM((2, tm, CM_DIM), BF16)],
        compiler_params=pltpu.CompilerParams(
            dimension_semantics=("arbitrary",), vmem_limit_bytes=VMEM_LIMIT),
        name="mix_ffn",
    )(*x_args, yf, yb, z, u, mods, ssd_norm_g, norm2_g, final_g, cm_w_rows, cm_b, ln_g, ln_b,
      w_out_b, w1_b, w2_b)


def kernel(x, c, ctx, c_ctx, w_in, ssd_conv_w, ssd_conv_b, dt_bias, a_log, d_skip, ssd_norm_g,
           cm_dw_w, cm_dw_b, cm_ln_g, cm_ln_b, w_out, w_ffn_in, w_ffn_out, ada_w, ada_b,
           norm1_g, norm2_g, final_norm_g):
    n_batch, lat_len, _ = x.shape
    ctx_len = ctx.shape[1]
    depth = w_in.shape[0]
    n_lat_rows = n_batch * lat_len
    assert lat_len % TM_IN == 0 and (n_batch * ctx_len) % TM_IN == 0
    assert lat_len % CHUNK == 0 and ctx_len % CHUNK == 0 and n_batch + 1 <= 16

    all_rows = n_lat_rows + n_batch * ctx_len
    x_lat, x_ctx, ctx_row0 = x.reshape(n_lat_rows, D_MODEL), ctx.reshape(n_batch * ctx_len, D_MODEL), 0

    c_all = jnp.concatenate([c, c_ctx[None, :],
                             jnp.zeros((16 - n_batch - 1, D_MODEL), F32)], axis=0)
    mods = _mods(c_all, ada_w, ada_b).reshape(depth, 16, 1, 6 * D_MODEL)

    i0, i1, i2 = SSD_DIM, SSD_DIM + XBC_DIM, SSD_DIM + XBC_DIM + 2 * SSD_HEADS
    w_in_b = jnp.concatenate(
        [w_in[..., :i1], w_in[..., i2:], w_in[..., i1:i2],
         jnp.zeros((depth, D_MODEL, DT_PAD - 2 * SSD_HEADS), F32)], axis=-1).astype(BF16)
    w_out_b = w_out.astype(BF16)
    w1_b = w_ffn_in.astype(BF16)
    w2_b = w_ffn_out.astype(BF16)
    pad_lanes = lambda v: jnp.pad(v.reshape(depth, 1, 2 * SSD_HEADS),
                                  ((0, 0), (0, 0), (0, DT_PAD - 2 * SSD_HEADS)))
    dt_bias_p = pad_lanes(dt_bias)
    a_log_p = pad_lanes(a_log)
    d_skip_exp = jnp.repeat(d_skip, SSD_HEAD_DIM, axis=-1).reshape(depth, 1, SSD_DIM)
    row3 = lambda v: v.reshape(depth, 1, v.shape[-1])
    cm_w_rows = jnp.pad(cm_dw_w, ((0, 0), (CM_TAP_ROWS - CM_KERNEL, 0), (0, 0))).reshape(
        depth, CM_TAP_ROWS, 1, CM_DIM)

    r = jnp.arange(2 * LANES)[:, None] % LANES
    col_head = jnp.arange(SSD_DIM)[None, :] // SSD_HEAD_DIM
    expand = jnp.stack([(r == col_head), (r == col_head + SSD_HEADS)]).astype(BF16)

    for i in range(depth):
        final = i == depth - 1
        z, xbc, u, dt = _inproj(x_lat, x_ctx, ctx_row0, all_rows, mods, row3(norm1_g), w_in_b, i,
                                n_lat_rows, lat_len)
        yf, yb = _ssd(xbc, dt, ssd_conv_w, row3(ssd_conv_b), dt_bias_p, a_log_p, d_skip_exp,
                      expand, i, n_batch, lat_len, ctx_len)
        xall = _mixffn(x_lat, x_ctx, ctx_row0, all_rows, yf, yb, z, u, mods, row3(ssd_norm_g),
                       row3(norm2_g), final_norm_g.reshape(1, D_MODEL), cm_w_rows, row3(cm_dw_b),
                       row3(cm_ln_g), row3(cm_ln_b), w_out_b, w1_b, w2_b,
                       i, n_lat_rows, lat_len, ctx_len, final)
        x_lat, x_ctx, ctx_row0 = xall, xall, n_lat_rows
    return xall.reshape(n_batch, lat_len, D_MODEL)
```
